```python
import jax, jax.numpy as jnp
from jax import lax
import numpy as np

D_MODEL = 1024
BATCH = 8
SEQ = 4096
DEPTH = 2

EXPAND = 2
WIDTH = EXPAND * D_MODEL
HEAD_DIM = 128
A_HEADS = WIDTH // HEAD_DIM
B_HEADS = WIDTH // HEAD_DIM
N_A = DEPTH // 2
N_B = DEPTH - N_A
CHUNK = 64
Q_BLOCK = 128
EPS = 1e-6

kernel_name = "yoco_hgrn2_fox_hybrid"


def _rms(x, g):
    xf = x.astype(jnp.float32)
    y = xf * lax.rsqrt(jnp.mean(xf * xf, axis=-1, keepdims=True) + EPS)
    return (y * g.astype(jnp.float32)).astype(x.dtype)


def _ada(c, w, b):
    return jax.nn.silu(c) @ w + b


def _hgrn2(h, w_in, lb, onorm_g, w_out):
    bsz, s, _ = h.shape
    f32 = jnp.float32
    q, fz, i, g = jnp.split(h @ w_in, 4, axis=-1)
    fz = fz.astype(f32)
    logf = jnp.log(lb + (1.0 - lb) * jax.nn.sigmoid(fz))
    k = (1.0 - lb) * jax.nn.sigmoid(-fz)

    def hc(t):
        return t.reshape(bsz, s // CHUNK, CHUNK, A_HEADS, HEAD_DIM).transpose(0, 3, 1, 2, 4)

    q, k, v, logf = hc(q.astype(f32)), hc(k), hc(i.astype(f32)), hc(logf)
    b = jnp.cumsum(logf, axis=3)
    b_last = b[:, :, :, -1:, :]
    q_dec = q * jnp.exp(b)
    k_in = k * jnp.exp(-b)
    k_st = k * jnp.exp(b_last - b)
    causal = jnp.tril(jnp.ones((CHUNK, CHUNK), dtype=bool))
    att = jnp.where(causal, jnp.einsum('bhnck,bhnsk->bhncs', q_dec, k_in), 0.0)
    o_intra = jnp.einsum('bhncs,bhnsv->bhncv', att, v)

    def step(state, xs):
        qd, ks, vv, dl = xs
        o = jnp.einsum('bhck,bhkv->bhcv', qd, state)
        state = dl[..., None] * state + jnp.einsum('bhck,bhcv->bhkv', ks, vv)
        return state, o

    xs = (jnp.moveaxis(q_dec, 2, 0), jnp.moveaxis(k_st, 2, 0), jnp.moveaxis(v, 2, 0),
          jnp.moveaxis(jnp.exp(b_last[:, :, :, 0, :]), 2, 0))
    s0 = jnp.zeros((bsz, A_HEADS, HEAD_DIM, HEAD_DIM), f32)
    _, o_inter = lax.scan(step, s0, xs)
    o = o_intra + jnp.moveaxis(o_inter, 0, 2)
    o = o.transpose(0, 2, 3, 1, 4).reshape(bsz, s, A_HEADS, HEAD_DIM)
    o = _rms(o, onorm_g.reshape(A_HEADS, HEAD_DIM)).reshape(bsz, s, WIDTH).astype(h.dtype)
    return (o * jax.nn.silu(g)) @ w_out


def _shared_kv(x, c, kv_mod_w, kv_mod_b, kv_norm_g, kv_w, kv_fb, k_norm_g):
    bsz, s, _ = x.shape
    shift, scale = jnp.split(_ada(c, kv_mod_w, kv_mod_b), 2, axis=-1)
    h = _rms(x, kv_norm_g) * (1.0 + scale[:, None]) + shift[:, None]
    proj = h @ kv_w
    k = proj[..., :WIDTH].reshape(bsz, s, B_HEADS, HEAD_DIM)
    v = proj[..., WIDTH:2 * WIDTH].reshape(bsz, s, B_HEADS, HEAD_DIM)
    fl = proj[..., 2 * WIDTH:] + kv_fb
    k = _rms(k, k_norm_g).transpose(0, 2, 1, 3)
    v = v.transpose(0, 2, 1, 3)
    F = jnp.cumsum(jax.nn.log_sigmoid(fl.astype(jnp.float32)), axis=1).transpose(0, 2, 1)
    return k, v, F


def _fox(h, w_in, q_norm_g, w_out, k, v, F):
    bsz, s, _ = h.shape
    q, g = jnp.split(h @ w_in, 2, axis=-1)
    q = _rms(q.reshape(bsz, s, B_HEADS, HEAD_DIM), q_norm_g).transpose(0, 2, 1, 3)
    scale = HEAD_DIM ** -0.5
    outs = []
    for blk in range(s // Q_BLOCK):
        lo, hi = blk * Q_BLOCK, (blk + 1) * Q_BLOCK
        logits = (jnp.einsum('bhqd,bhkd->bhqk', q[:, :, lo:hi], k[:, :, :hi]).astype(jnp.float32) * scale
                  + (F[:, :, lo:hi, None] - F[:, :, None, :hi]))
        mask = (lo + jnp.arange(Q_BLOCK))[:, None] >= jnp.arange(hi)[None, :]
        p = jax.nn.softmax(jnp.where(mask, logits, -jnp.inf), axis=-1)
        outs.append(jnp.einsum('bhqk,bhkd->bhqd', p.astype(v.dtype), v[:, :, :hi]))
    o = jnp.concatenate(outs, axis=2).transpose(0, 2, 1, 3).reshape(bsz, s, WIDTH)
    return (o * jax.nn.silu(g)) @ w_out


def setup_inputs(seed: int = 0) -> dict:
    key = jax.random.key(seed)
    ks = jax.random.split(key, 20)
    n = jax.random.normal
    D, W = D_MODEL, WIDTH
    return {
        "x": n(ks[0], (BATCH, SEQ, D), jnp.float32),
        "c": n(ks[1], (BATCH, D), jnp.float32),
        "mod_w": n(ks[2], (DEPTH, D, 3 * D), jnp.float32) * (0.3 * D ** -0.5),
        "mod_b": n(ks[3], (DEPTH, 3 * D), jnp.float32) * 0.02,
        "norm_g": 1.0 + 0.02 * n(ks[4], (DEPTH, D), jnp.float32),
        "a_w_in": n(ks[5], (N_A, D, 4 * W), jnp.float32) * D ** -0.5,
        "a_lb_logits": 0.5 * n(ks[6], (N_A + 1, W), jnp.float32),
        "a_onorm_g": 1.0 + 0.02 * n(ks[7], (N_A, W), jnp.float32),
        "a_w_out": n(ks[8], (N_A, W, D), jnp.float32) * W ** -0.5,
        "kv_mod_w": n(ks[9], (D, 2 * D), jnp.float32) * (0.3 * D ** -0.5),
        "kv_mod_b": n(ks[10], (2 * D,), jnp.float32) * 0.02,
        "kv_norm_g": 1.0 + 0.02 * n(ks[11], (D,), jnp.float32),
        "kv_w": n(ks[12], (D, 2 * W + B_HEADS), jnp.float32) * D ** -0.5,
        "kv_fb": 2.0 + 0.5 * n(ks[13], (B_HEADS,), jnp.float32),
        "k_norm_g": 1.0 + 0.02 * n(ks[14], (HEAD_DIM,), jnp.float32),
        "b_w_in": n(ks[15], (N_B, D, 2 * W), jnp.float32) * D ** -0.5,
        "b_q_norm_g": 1.0 + 0.02 * n(ks[16], (N_B, HEAD_DIM), jnp.float32),
        "b_w_out": n(ks[17], (N_B, W, D), jnp.float32) * W ** -0.5,
    }


def reference(x, c, mod_w, mod_b, norm_g, a_w_in, a_lb_logits, a_onorm_g, a_w_out,
              kv_mod_w, kv_mod_b, kv_norm_g, kv_w, kv_fb, k_norm_g,
              b_w_in, b_q_norm_g, b_w_out):
    lb_all = jnp.cumsum(jax.nn.softmax(a_lb_logits.astype(jnp.float32), axis=0), axis=0)
    kv = None
    for l in range(DEPTH):
        shift, scale, gate = jnp.split(_ada(c, mod_w[l], mod_b[l]), 3, axis=-1)
        h = _rms(x, norm_g[l]) * (1.0 + scale[:, None]) + shift[:, None]
        if l < N_A:
            y = _hgrn2(h, a_w_in[l], lb_all[l], a_onorm_g[l], a_w_out[l])
        else:
            if l == N_A:
                kv = _shared_kv(x, c, kv_mod_w, kv_mod_b, kv_norm_g, kv_w, kv_fb, k_norm_g)
            j = l - N_A
            y = _fox(h, b_w_in[j], b_q_norm_g[j], b_w_out[j], *kv)
        x = x + gate[:, None] * y
    return x
```

```python
import functools

import jax
import jax.numpy as jnp
from jax import lax
from jax.experimental import pallas as pl
from jax.experimental.pallas import tpu as pltpu

F32 = jnp.float32
BF16 = jnp.bfloat16

HEAD_DIM = 128
CHUNK = 64
EPS = 1e-6
V7X_VMEM_LIMIT_BYTES = 56 * 1024 * 1024

HG_TM = 256
PROJ_TM = 512
ATT_TQ = 512
OUT_TM = 512


def _sigmoid(x):
    return 1.0 / (1.0 + jnp.exp(-x))


def _const_spec(shape):
    zeros = (0,) * len(shape)
    return pl.BlockSpec(shape, lambda *_: zeros, pipeline_mode=pl.Buffered(1))


def _mod_kernel(c_ref, w_ref, b_ref, o_ref):
    c = c_ref[...]
    s = c * _sigmoid(c)
    o_ref[...] = jnp.dot(s, w_ref[...], preferred_element_type=F32,
                         precision=lax.Precision.HIGHEST) + b_ref[...]


def _mod_call(c, w, b):
    bsz, d = c.shape
    n = w.shape[1]
    tn = 1024
    return pl.pallas_call(
        _mod_kernel,
        grid=(n // tn,),
        in_specs=[pl.BlockSpec((bsz, d), lambda j: (0, 0)),
                  pl.BlockSpec((d, tn), lambda j: (0, j)),
                  pl.BlockSpec((1, tn), lambda j: (0, j))],
        out_specs=pl.BlockSpec((bsz, tn), lambda j: (0, j)),
        out_shape=jax.ShapeDtypeStruct((bsz, n), F32),
        name="mod",
    )(c, w, b.reshape(1, n))


def _chunk_cumsum(x, rows_in_chunk):
    s = 1
    while s < CHUNK:
        x = x + jnp.where(rows_in_chunk >= s, pltpu.roll(x, s, 0), 0.0)
        s *= 2
    return x


def _hgrn2_kernel(x_ref, mod_ref, ng_ref, win_ref, lbl_ref, og_ref, wout_ref,
                  o_ref, st_ref, h_ref, gated_ref, *, tm, d, n_pairs):
    i = pl.program_id(1)
    n_chunks = tm // CHUNK
    pw = 2 * HEAD_DIM

    @pl.when(i == 0)
    def _():
        st_ref[...] = jnp.zeros_like(st_ref)

    x = x_ref[0]
    shift = mod_ref[0, :, 0:d]
    scale = mod_ref[0, :, d:2 * d]
    gate = mod_ref[0, :, 2 * d:3 * d]
    xn = x * lax.rsqrt(jnp.mean(x * x, axis=-1, keepdims=True) + EPS) * ng_ref[...]
    h_ref[...] = (xn * (1.0 + scale) + shift).astype(BF16)

    rows_in_chunk = lax.broadcasted_iota(jnp.int32, (tm, pw), 0) % CHUNK
    tril = (lax.broadcasted_iota(jnp.int32, (CHUNK, CHUNK), 0)
            >= lax.broadcasted_iota(jnp.int32, (CHUNK, CHUNK), 1))

    def pair_body(p, carry):
        proj = jnp.dot(h_ref[...], win_ref[p], preferred_element_type=F32)
        q = proj[:, 0:pw]
        fz = proj[:, pw:2 * pw]
        v = proj[:, 2 * pw:3 * pw]
        g = proj[:, 3 * pw:4 * pw]

        l0 = lbl_ref[0, p]
        l1 = lbl_ref[1, p]
        mx = jnp.maximum(l0, l1)
        e0 = jnp.exp(l0 - mx)
        e1 = jnp.exp(l1 - mx)
        lb = e0 / (e0 + e1)

        logf = jnp.log(lb + (1.0 - lb) * _sigmoid(fz))
        k = (1.0 - lb) * _sigmoid(-fz)
        b = _chunk_cumsum(logf, rows_in_chunk)
        b3 = b.reshape(n_chunks, CHUNK, pw)
        b_last = b3[:, CHUNK - 1:CHUNK, :]
        qd = (q * jnp.exp(b)).astype(BF16)
        ki = (k * jnp.exp(-b)).astype(BF16)
        ks = (k.reshape(n_chunks, CHUNK, pw) * jnp.exp(b_last - b3)).astype(BF16)
        dl = jnp.exp(b_last)

        normed = []
        for hh in range(2):
            sl = slice(hh * HEAD_DIM, (hh + 1) * HEAD_DIM)
            st = st_ref[2 * p + hh]
            outs = []
            for c in range(n_chunks):
                rows = slice(c * CHUNK, (c + 1) * CHUNK)
                qd_c = qd[rows, sl]
                v_c = v[rows, sl]
                att = lax.dot_general(qd_c, ki[rows, sl], (((1,), (1,)), ((), ())),
                                      preferred_element_type=F32)
                att = jnp.where(tril, att, 0.0).astype(BF16)
                o_c = jnp.dot(att, v_c.astype(BF16), preferred_element_type=F32)
                o_c = o_c + lax.dot_general(qd_c, st.astype(BF16), (((1,), (1,)), ((), ())),
                                            preferred_element_type=F32)
                d_st = jnp.dot(v_c.T.astype(BF16), ks[c][:, sl], preferred_element_type=F32)
                st = st * dl[c][:, sl] + d_st
                outs.append(o_c)
            st_ref[2 * p + hh] = st
            o_h = jnp.concatenate(outs, axis=0)
            o_h = o_h * lax.rsqrt(jnp.mean(o_h * o_h, axis=-1, keepdims=True) + EPS)
            normed.append(o_h)
        o_pair = jnp.concatenate(normed, axis=1) * og_ref[p]
        gated_ref[p] = (o_pair * (g * _sigmoid(g))).astype(BF16)
        return carry

    lax.fori_loop(0, n_pairs, pair_body, 0)

    y = jnp.dot(gated_ref[0], wout_ref[0], preferred_element_type=F32)
    for p in range(1, n_pairs):
        y = y + jnp.dot(gated_ref[p], wout_ref[p], preferred_element_type=F32)
    o_ref[0] = x + gate * y


def _hgrn2_call(x, mod, norm_g, w_in, lb_logits, onorm_g, w_out):
    bsz, s, d = x.shape
    w = w_out.shape[0]
    n_heads = w // HEAD_DIM
    n_pairs = n_heads // 2
    pw = 2 * HEAD_DIM
    tm = HG_TM
    w_in_p = (w_in.reshape(d, 4, n_pairs, pw).transpose(2, 0, 1, 3)
              .reshape(n_pairs, d, 4 * pw).astype(BF16))
    w_out_p = w_out.reshape(n_pairs, pw, d).astype(BF16)
    kern = functools.partial(_hgrn2_kernel, tm=tm, d=d, n_pairs=n_pairs)
    return pl.pallas_call(
        kern,
        grid=(bsz, s // tm),
        in_specs=[pl.BlockSpec((1, tm, d), lambda b, i: (b, i, 0)),
                  pl.BlockSpec((1, 1, 3 * d), lambda b, i: (b, 0, 0)),
                  _const_spec((1, d)),
                  _const_spec((n_pairs, d, 4 * pw)),
                  _const_spec((2, n_pairs, 1, pw)),
                  _const_spec((n_pairs, 1, pw)),
                  _const_spec((n_pairs, pw, d))],
        out_specs=pl.BlockSpec((1, tm, d), lambda b, i: (b, i, 0)),
        out_shape=jax.ShapeDtypeStruct((bsz, s, d), F32),
        scratch_shapes=[pltpu.VMEM((n_heads, HEAD_DIM, HEAD_DIM), F32),
                        pltpu.VMEM((tm, d), BF16),
                        pltpu.VMEM((n_pairs, tm, pw), BF16)],
        compiler_params=pltpu.CompilerParams(
            dimension_semantics=("parallel", "arbitrary"),
            vmem_limit_bytes=V7X_VMEM_LIMIT_BYTES),
        name="hgrn2_layer",
    )(x, mod.reshape(bsz, 1, 3 * d), norm_g.reshape(1, d), w_in_p,
      lb_logits.reshape(2, n_pairs, 1, pw), onorm_g.reshape(n_pairs, 1, pw), w_out_p)


def _row_cumsum(x, n_rows):
    rows = lax.broadcasted_iota(jnp.int32, x.shape, 0)
    s = 1
    while s < n_rows:
        x = x + jnp.where(rows >= s, pltpu.roll(x, s, 0), 0.0)
        s *= 2
    return x


def _fox_proj_kernel(x_ref, mod_ref, kvmod_ref, ng_ref, kvng_ref, wq_ref, wkv_ref, wf_ref,
                     fb_ref, qg_ref, kg_ref,
                     q_ref, sg_ref, k_ref, v_ref, f_ref, carry_ref, *, tm, d, w, nb):
    i = pl.program_id(1)

    @pl.when(i == 0)
    def _():
        carry_ref[...] = jnp.zeros_like(carry_ref)

    x = x_ref[0]
    xn = x * lax.rsqrt(jnp.mean(x * x, axis=-1, keepdims=True) + EPS)
    h = ((xn * ng_ref[...]) * (1.0 + mod_ref[0, :, d:2 * d]) + mod_ref[0, :, 0:d]).astype(BF16)
    hkv = (xn * kvng_ref[...]) * (1.0 + kvmod_ref[0, :, d:2 * d]) + kvmod_ref[0, :, 0:d]
    hkv_hi = hkv.astype(BF16)
    hkv_lo = (hkv - hkv_hi.astype(F32)).astype(BF16)

    def head_norm(t, gain):
        parts = []
        for j in range(nb // HEAD_DIM):
            th = t[:, j * HEAD_DIM:(j + 1) * HEAD_DIM]
            parts.append(th * lax.rsqrt(jnp.mean(th * th, axis=-1, keepdims=True) + EPS) * gain)
        return jnp.concatenate(parts, axis=1)

    q_gain = qg_ref[...] * (HEAD_DIM ** -0.5)
    for j in range(w // nb):
        cols = slice(j * nb, (j + 1) * nb)
        qj = jnp.dot(h, wq_ref[:, cols], preferred_element_type=F32)
        q_ref[0, :, cols] = head_norm(qj, q_gain).astype(BF16)
        gj = jnp.dot(h, wq_ref[:, w + j * nb:w + (j + 1) * nb], preferred_element_type=F32)
        sg_ref[0, :, cols] = (gj * _sigmoid(gj)).astype(BF16)
        kj = jnp.dot(hkv_hi, wkv_ref[:, cols], preferred_element_type=F32)
        k_ref[0, :, cols] = head_norm(kj, kg_ref[...]).astype(BF16)
        vj = jnp.dot(hkv_hi, wkv_ref[:, w + j * nb:w + (j + 1) * nb], preferred_element_type=F32)
        v_ref[0, :, cols] = vj.astype(BF16)

    r1 = jnp.dot(hkv_hi, wf_ref[...], preferred_element_type=F32)
    r2 = jnp.dot(hkv_lo, wf_ref[:, 0:HEAD_DIM], preferred_element_type=F32)
    fl = r1[:, 0:HEAD_DIM] + r1[:, HEAD_DIM:2 * HEAD_DIM] + r2 + fb_ref[...]
    log_sig = jnp.minimum(fl, 0.0) - jnp.log(1.0 + jnp.exp(-jnp.abs(fl)))
    cum = _row_cumsum(log_sig, tm) + carry_ref[...]
    carry_ref[...] = cum[tm - 1:tm, :]
    f_ref[0] = cum[:, 0:f_ref.shape[2]]


def _fox_proj_call(x, mod, kvmod, norm_g, kv_norm_g, w_in, kv_w, kv_fb, q_norm_g, k_norm_g):
    bsz, s, d = x.shape
    w = w_in.shape[1] // 2
    n_heads = w // HEAD_DIM
    tm = PROJ_TM
    nb = 512
    wq = w_in.astype(BF16)
    wkv = kv_w[:, :2 * w].astype(BF16)
    wf = kv_w[:, 2 * w:]
    wf_hi = wf.astype(BF16)
    wf_lo = (wf - wf_hi.astype(F32)).astype(BF16)
    pad = jnp.zeros((d, HEAD_DIM - n_heads), BF16)
    wf_p = jnp.concatenate([wf_hi, pad, wf_lo, pad], axis=1)
    fb_p = jnp.concatenate([kv_fb, jnp.zeros((HEAD_DIM - n_heads,), F32)]).reshape(1, HEAD_DIM)
    kern = functools.partial(_fox_proj_kernel, tm=tm, d=d, w=w, nb=nb)
    tok = lambda b, i: (b, i, 0)
    per_b = lambda b, i: (b, 0, 0)
    act = jax.ShapeDtypeStruct((bsz, s, w), BF16)
    return pl.pallas_call(
        kern,
        grid=(bsz, s // tm),
        in_specs=[pl.BlockSpec((1, tm, d), tok),
                  pl.BlockSpec((1, 1, 3 * d), per_b),
                  pl.BlockSpec((1, 1, 2 * d), per_b),
                  _const_spec((1, d)), _const_spec((1, d)),
                  _const_spec((d, 2 * w)), _const_spec((d, 2 * w)),
                  _const_spec((d, 2 * HEAD_DIM)),
                  _const_spec((1, HEAD_DIM)), _const_spec((1, HEAD_DIM)), _const_spec((1, HEAD_DIM))],
        out_specs=[pl.BlockSpec((1, tm, w), tok)] * 4 + [pl.BlockSpec((1, tm, n_heads), tok)],
        out_shape=[act, act, act, act, jax.ShapeDtypeStruct((bsz, s, n_heads), F32)],
        scratch_shapes=[pltpu.VMEM((1, HEAD_DIM), F32)],
        compiler_params=pltpu.CompilerParams(
            dimension_semantics=("parallel", "arbitrary"),
            vmem_limit_bytes=V7X_VMEM_LIMIT_BYTES),
        name="fox_proj",
    )(x, mod.reshape(bsz, 1, 3 * d), kvmod.reshape(bsz, 1, 2 * d), norm_g.reshape(1, d),
      kv_norm_g.reshape(1, d), wq, wkv, wf_p, fb_p, q_norm_g.reshape(1, HEAD_DIM),
      k_norm_g.reshape(1, HEAD_DIM))


def _fox_attn_kernel(q_ref, k_ref, v_ref, frow_ref, ftok_ref, sg_ref, o_ref, *, tq):
    h = pl.program_id(1)
    i = pl.program_id(2)
    q = q_ref[0]
    ftok = ftok_ref[0]
    lane = lax.broadcasted_iota(jnp.int32, ftok.shape, 1)
    f_t = jnp.sum(jnp.where(lane == h, ftok, 0.0), axis=1, keepdims=True)

    def block(j, m, l, acc, masked):
        start = pl.multiple_of(j * tq, tq)
        kb = k_ref[0, pl.ds(start, tq), :]
        vb = v_ref[0, pl.ds(start, tq), :]
        s = lax.dot_general(q, kb, (((1,), (1,)), ((), ())), preferred_element_type=F32)
        s = s + (f_t - frow_ref[0, 0, pl.ds(j, 1), :])
        if masked:
            row = lax.broadcasted_iota(jnp.int32, (tq, tq), 0)
            col = lax.broadcasted_iota(jnp.int32, (tq, tq), 1)
            s = jnp.where(row >= col, s, -jnp.inf)
        m_new = jnp.maximum(m, jnp.max(s, axis=1, keepdims=True))
        alpha = jnp.exp(m - m_new)
        p = jnp.exp(s - m_new)
        l = alpha * l + jnp.sum(p, axis=1, keepdims=True)
        acc = alpha * acc + jnp.dot(p.astype(BF16), vb, preferred_element_type=F32)
        return m_new, l, acc

    m0 = jnp.full((tq, 1), -jnp.inf, F32)
    l0 = jnp.zeros((tq, 1), F32)
    acc0 = jnp.zeros((tq, HEAD_DIM), F32)
    m, l, acc = block(i, m0, l0, acc0, True)
    m, l, acc = lax.fori_loop(0, i, lambda j, c: block(j, *c, False), (m, l, acc))
    o_ref[0] = (acc / l * sg_ref[0].astype(F32)).astype(BF16)


def _fox_attn_call(q, k, v, f_tok, sg):
    bsz, s, w = q.shape
    n_heads = w // HEAD_DIM
    tq = ATT_TQ
    f_row = jnp.transpose(f_tok, (0, 2, 1)).reshape(bsz, n_heads, s // tq, tq)
    kern = functools.partial(_fox_attn_kernel, tq=tq)
    qspec = pl.BlockSpec((1, tq, HEAD_DIM), lambda b, h, i: (b, i, h))
    kvspec = pl.BlockSpec((1, s, HEAD_DIM), lambda b, h, i: (b, 0, h))
    return pl.pallas_call(
        kern,
        grid=(bsz, n_heads, s // tq),
        in_specs=[qspec, kvspec, kvspec,
                  pl.BlockSpec((1, 1, s // tq, tq), lambda b, h, i: (b, h, 0, 0)),
                  pl.BlockSpec((1, tq, n_heads), lambda b, h, i: (b, i, 0)),
                  qspec],
        out_specs=qspec,
        out_shape=jax.ShapeDtypeStruct((bsz, s, w), BF16),
        compiler_params=pltpu.CompilerParams(
            dimension_semantics=("parallel", "parallel", "arbitrary"),
            vmem_limit_bytes=V7X_VMEM_LIMIT_BYTES),
        name="fox_attn",
    )(q, k, v, f_row, f_tok, sg)


def _out_proj_kernel(x_ref, a_ref, w_ref, mod_ref, o_ref, *, d):
    y = jnp.dot(a_ref[0], w_ref[...], preferred_element_type=F32)
    o_ref[0] = x_ref[0] + mod_ref[0, :, 2 * d:3 * d] * y


def _out_proj_call(x, a, w_out, mod):
    bsz, s, d = x.shape
    w = a.shape[2]
    tm = OUT_TM
    tok = lambda b, i: (b, i, 0)
    return pl.pallas_call(
        functools.partial(_out_proj_kernel, d=d),
        grid=(bsz, s // tm),
        in_specs=[pl.BlockSpec((1, tm, d), tok),
                  pl.BlockSpec((1, tm, w), tok),
                  _const_spec((w, d)),
                  pl.BlockSpec((1, 1, 3 * d), lambda b, i: (b, 0, 0))],
        out_specs=pl.BlockSpec((1, tm, d), tok),
        out_shape=jax.ShapeDtypeStruct((bsz, s, d), F32),
        compiler_params=pltpu.CompilerParams(
            dimension_semantics=("parallel", "parallel"),
            vmem_limit_bytes=V7X_VMEM_LIMIT_BYTES),
        name="out_proj",
    )(x, a, w_out.astype(BF16), mod.reshape(bsz, 1, 3 * d))


def kernel(x, c, mod_w, mod_b, norm_g, a_w_in, a_lb_logits, a_onorm_g, a_w_out,
           kv_mod_w, kv_mod_b, kv_norm_g, kv_w, kv_fb, k_norm_g,
           b_w_in, b_q_norm_g, b_w_out):
    assert mod_w.shape[0] == 2 and a_w_in.shape[0] == 1 and b_w_in.shape[0] == 1
    mod0 = _mod_call(c, mod_w[0], mod_b[0])
    mod1 = _mod_call(c, mod_w[1], mod_b[1])
    kvmod = _mod_call(c, kv_mod_w, kv_mod_b)

    x1 = _hgrn2_call(x, mod0, norm_g[0], a_w_in[0], a_lb_logits, a_onorm_g[0], a_w_out[0])

    q, sg, k, v, f_tok = _fox_proj_call(x1, mod1, kvmod, norm_g[1], kv_norm_g, b_w_in[0], kv_w,
                                        kv_fb, b_q_norm_g[0], k_norm_g)
    gated = _fox_attn_call(q, k, v, f_tok, sg)
    return _out_proj_call(x1, gated, b_w_out[0], mod1)
```

```python
import functools

import jax
import jax.numpy as jnp
import numpy as np
from jax import lax
from jax.experimental import pallas as pl
from jax.experimental.pallas import tpu as pltpu

F32 = jnp.float32
BF16 = jnp.bfloat16

HEAD_DIM = 128
CHUNK = 64
EPS = 1e-6
LOG2E = 1.4426950408889634
V7X_VMEM_LIMIT_BYTES = 56 * 1024 * 1024

HG_TM = 256
PROJ_TM = 512
ATT_TQ = 512
ATT_HEADS = 2
OUT_TM = 512


def _sigmoid(x):
    return 0.5 * jnp.tanh(0.5 * x) + 0.5


def _const_spec(shape):
    zeros = (0,) * len(shape)
    return pl.BlockSpec(shape, lambda *_: zeros, pipeline_mode=pl.Buffered(1))


def _mod_kernel(c_ref, w_ref, b_ref, o_ref):
    c = c_ref[...]
    s = c * _sigmoid(c)
    o_ref[...] = jnp.dot(s, w_ref[...], preferred_element_type=F32,
                         precision=lax.Precision.HIGHEST) + b_ref[...]


def _mod_call(c, w, b):
    bsz, d = c.shape
    n = w.shape[1]
    tn = 1024
    return pl.pallas_call(
        _mod_kernel,
        grid=(n // tn,),
        in_specs=[pl.BlockSpec((bsz, d), lambda j: (0, 0)),
                  pl.BlockSpec((d, tn), lambda j: (0, j)),
                  pl.BlockSpec((1, tn), lambda j: (0, j))],
        out_specs=pl.BlockSpec((bsz, tn), lambda j: (0, j)),
        out_shape=jax.ShapeDtypeStruct((bsz, n), F32),
        name="mod",
    )(c, w, b.reshape(1, n))


def _chunk_cumsum(x, rows_in_chunk):
    s = 1
    while s < CHUNK:
        x = x + jnp.where(rows_in_chunk >= s, pltpu.roll(x, s, 0), 0.0)
        s *= 2
    return x


def _hgrn2_kernel(x_ref, mod_ref, ng_ref, win_ref, lbl_ref, og_ref, wout_ref,
                  o_ref, st_ref, h_ref, gated_ref, pa_ref, pb_ref,
                  qd_ref, ki_ref, ks_ref, v_ref, dl_ref, sg_ref, *, tm, d, n_pairs):
    i = pl.program_id(1)
    n_chunks = tm // CHUNK
    pw = 2 * HEAD_DIM

    @pl.when(i == 0)
    def _():
        st_ref[...] = jnp.zeros_like(st_ref)

    x = x_ref[0]
    shift = mod_ref[0, :, 0:d]
    scale = mod_ref[0, :, d:2 * d]
    gate = mod_ref[0, :, 2 * d:3 * d]
    xn = x * lax.rsqrt(jnp.mean(x * x, axis=-1, keepdims=True) + EPS) * ng_ref[...]
    h_ref[...] = (xn * (1.0 + scale) + shift).astype(BF16)

    rows_in_chunk = lax.broadcasted_iota(jnp.int32, (tm, pw), 0) % CHUNK
    tril = (lax.broadcasted_iota(jnp.int32, (CHUNK, CHUNK), 0)
            >= lax.broadcasted_iota(jnp.int32, (CHUNK, CHUNK), 1))

    def in_proj(p, dst_ref):
        dst_ref[...] = jnp.dot(h_ref[...], win_ref[p], preferred_element_type=F32)

    def gates(p, proj_ref):
        q = proj_ref[:, 0:pw]
        fz = proj_ref[:, pw:2 * pw]
        g = proj_ref[:, 3 * pw:4 * pw]

        l0 = lbl_ref[0, p]
        l1 = lbl_ref[1, p]
        mx = jnp.maximum(l0, l1)
        e0 = jnp.exp(l0 - mx)
        e1 = jnp.exp(l1 - mx)
        lb = e0 / (e0 + e1)

        sig = _sigmoid(fz)
        logf = jnp.log(lb + (1.0 - lb) * sig)
        k = (1.0 - lb) * (1.0 - sig)
        b = _chunk_cumsum(logf, rows_in_chunk)
        b3 = b.reshape(n_chunks, CHUNK, pw)
        b_last = b3[:, CHUNK - 1:CHUNK, :]
        qd_ref[...] = (q * jnp.exp(b)).astype(BF16)
        ki_ref[...] = (k * jnp.exp(-b)).astype(BF16)
        ks = k.reshape(n_chunks, CHUNK, pw) * jnp.exp(b_last - b3)
        ks_ref[...] = ks.reshape(tm, pw).astype(BF16)
        v_ref[...] = proj_ref[:, 2 * pw:3 * pw].astype(BF16)
        dl_ref[...] = jnp.exp(b_last)
        sg_ref[...] = g * _sigmoid(g)

    def chunks(p):
        normed = []
        for hh in range(2):
            sl = slice(hh * HEAD_DIM, (hh + 1) * HEAD_DIM)
            st = st_ref[2 * p + hh]
            outs = []
            for c in range(n_chunks):
                rows = slice(c * CHUNK, (c + 1) * CHUNK)
                qd_c = qd_ref[rows, sl]
                v_c = v_ref[rows, sl]
                att = lax.dot_general(qd_c, ki_ref[rows, sl], (((1,), (1,)), ((), ())),
                                      preferred_element_type=F32)
                att = jnp.where(tril, att, 0.0).astype(BF16)
                o_c = jnp.dot(att, v_c, preferred_element_type=F32)
                o_c = o_c + lax.dot_general(qd_c, st.astype(BF16), (((1,), (1,)), ((), ())),
                                            preferred_element_type=F32)
                d_st = lax.dot_general(v_c, ks_ref[rows, sl], (((0,), (0,)), ((), ())),
                                       preferred_element_type=F32)
                st = st * dl_ref[c][:, sl] + d_st
                outs.append(o_c)
            st_ref[2 * p + hh] = st
            o_h = jnp.concatenate(outs, axis=0)
            o_h = o_h * lax.rsqrt(jnp.mean(o_h * o_h, axis=-1, keepdims=True) + EPS)
            normed.append(o_h)
        o_pair = jnp.concatenate(normed, axis=1) * og_ref[p]
        gated_ref[p] = (o_pair * sg_ref[...]).astype(BF16)

    def two_pairs(t, carry):
        in_proj(2 * t + 1, pb_ref)
        gates(2 * t, pa_ref)
        chunks(2 * t)
        in_proj(2 * t + 2, pa_ref)
        gates(2 * t + 1, pb_ref)
        chunks(2 * t + 1)
        return carry

    in_proj(0, pa_ref)
    lax.fori_loop(0, n_pairs // 2 - 1, two_pairs, 0)
    in_proj(n_pairs - 1, pb_ref)
    gates(n_pairs - 2, pa_ref)
    chunks(n_pairs - 2)
    gates(n_pairs - 1, pb_ref)
    chunks(n_pairs - 1)

    y = jnp.dot(gated_ref[0], wout_ref[0], preferred_element_type=F32)
    for p in range(1, n_pairs):
        y = y + jnp.dot(gated_ref[p], wout_ref[p], preferred_element_type=F32)
    o_ref[0] = x + gate * y


def _hgrn2_call(x, mod, norm_g, w_in, lb_logits, onorm_g, w_out):
    bsz, s, d = x.shape
    w = w_out.shape[0]
    n_heads = w // HEAD_DIM
    n_pairs = n_heads // 2
    pw = 2 * HEAD_DIM
    tm = HG_TM
    w_in_p = (w_in.reshape(d, 4, n_pairs, pw).transpose(2, 0, 1, 3)
              .reshape(n_pairs, d, 4 * pw).astype(BF16))
    w_out_p = w_out.reshape(n_pairs, pw, d).astype(BF16)
    kern = functools.partial(_hgrn2_kernel, tm=tm, d=d, n_pairs=n_pairs)
    return pl.pallas_call(
        kern,
        grid=(bsz, s // tm),
        in_specs=[pl.BlockSpec((1, tm, d), lambda b, i: (b, i, 0)),
                  pl.BlockSpec((1, 1, 3 * d), lambda b, i: (b, 0, 0)),
                  _const_spec((1, d)),
                  _const_spec((n_pairs, d, 4 * pw)),
                  _const_spec((2, n_pairs, 1, pw)),
                  _const_spec((n_pairs, 1, pw)),
                  _const_spec((n_pairs, pw, d))],
        out_specs=pl.BlockSpec((1, tm, d), lambda b, i: (b, i, 0)),
        out_shape=jax.ShapeDtypeStruct((bsz, s, d), F32),
        scratch_shapes=[pltpu.VMEM((n_heads, HEAD_DIM, HEAD_DIM), F32),
                        pltpu.VMEM((tm, d), BF16),
                        pltpu.VMEM((n_pairs, tm, pw), BF16),
                        pltpu.VMEM((tm, 4 * pw), F32),
                        pltpu.VMEM((tm, 4 * pw), F32),
                        pltpu.VMEM((tm, pw), BF16),
                        pltpu.VMEM((tm, pw), BF16),
                        pltpu.VMEM((tm, pw), BF16),
                        pltpu.VMEM((tm, pw), BF16),
                        pltpu.VMEM((tm // CHUNK, 1, pw), F32),
                        pltpu.VMEM((tm, pw), F32)],
        compiler_params=pltpu.CompilerParams(
            dimension_semantics=("parallel", "arbitrary"),
            vmem_limit_bytes=V7X_VMEM_LIMIT_BYTES),
        name="hgrn2_layer",
    )(x, mod.reshape(bsz, 1, 3 * d), norm_g.reshape(1, d), w_in_p,
      lb_logits.reshape(2, n_pairs, 1, pw), onorm_g.reshape(n_pairs, 1, pw), w_out_p)


N_SPLIT = 3


def _row_cumsum(x, n_rows):
    rows = lax.broadcasted_iota(jnp.int32, x.shape, 0)
    s = 1
    while s < n_rows:
        x = x + jnp.where(rows >= s, pltpu.roll(x, s, 0), 0.0)
        s *= 2
    return x


def _bias_selectors(n_heads):
    one_row = N_SPLIT * n_heads
    sel_k = np.zeros((HEAD_DIM, n_heads * HEAD_DIM), np.float32)
    sel_q = np.zeros((HEAD_DIM, n_heads * HEAD_DIM), np.float32)
    for h in range(n_heads):
        for r in range(N_SPLIT):
            sel_k[r * n_heads + h, h * HEAD_DIM + r] = -1.0
            sel_k[one_row, h * HEAD_DIM + N_SPLIT + r] = 1.0
            sel_q[one_row, h * HEAD_DIM + r] = 1.0
            sel_q[r * n_heads + h, h * HEAD_DIM + N_SPLIT + r] = 1.0
    return jnp.asarray(sel_q, BF16), jnp.asarray(sel_k, BF16)


def _fox_proj_kernel(x_ref, mod_ref, kvmod_ref, ng_ref, kvng_ref, wq_ref, wkv_ref, wf_ref,
                     fb_ref, qg_ref, kg_ref, selq_ref, selk_ref,
                     q_ref, sg_ref, k_ref, v_ref, carry_ref, *, tm, d, w, nb, n_heads):
    i = pl.program_id(1)

    @pl.when(i == 0)
    def _():
        carry_ref[...] = jnp.zeros_like(carry_ref)

    x = x_ref[0]
    xn = x * lax.rsqrt(jnp.mean(x * x, axis=-1, keepdims=True) + EPS)
    h = ((xn * ng_ref[...]) * (1.0 + mod_ref[0, :, d:2 * d]) + mod_ref[0, :, 0:d]).astype(BF16)
    hkv = (xn * kvng_ref[...]) * (1.0 + kvmod_ref[0, :, d:2 * d]) + kvmod_ref[0, :, 0:d]
    hkv_hi = hkv.astype(BF16)
    hkv_lo = (hkv - hkv_hi.astype(F32)).astype(BF16)

    r1 = jnp.dot(hkv_hi, wf_ref[...], preferred_element_type=F32)
    r2 = jnp.dot(hkv_lo, wf_ref[:, 0:HEAD_DIM], preferred_element_type=F32)
    fl = r1[:, 0:HEAD_DIM] + r1[:, HEAD_DIM:2 * HEAD_DIM] + r2 + fb_ref[...]
    log_sig = jnp.minimum(fl, 0.0) - jnp.log(1.0 + jnp.exp(-jnp.abs(fl)))
    cum = _row_cumsum(log_sig, tm) + carry_ref[...]
    carry_ref[...] = cum[tm - 1:tm, :]

    f2 = cum * LOG2E
    lane = lax.broadcasted_iota(jnp.int32, f2.shape, 1)
    split = jnp.where(lane == N_SPLIT * n_heads, 1.0, 0.0)
    rest = f2
    for r in range(N_SPLIT):
        piece = rest.astype(BF16).astype(F32)
        rest = rest - piece
        placed = piece if r == 0 else pltpu.roll(piece, r * n_heads, 1)
        split = jnp.where((lane >= r * n_heads) & (lane < (r + 1) * n_heads), placed, split)
    split = split.astype(BF16)
    q_bias = jnp.dot(split, selq_ref[...], preferred_element_type=F32).astype(BF16)
    k_bias = jnp.dot(split, selk_ref[...], preferred_element_type=F32).astype(BF16)

    def head_norm(t, gain):
        th = t * lax.rsqrt(jnp.mean(t * t, axis=-1, keepdims=True) + EPS) * gain
        return th.astype(BF16)

    q_gain = qg_ref[...] * (HEAD_DIM ** -0.5 * LOG2E)
    for j in range(w // nb):
        cols = slice(j * nb, (j + 1) * nb)
        qj = jnp.dot(h, wq_ref[:, cols], preferred_element_type=F32)
        gj = jnp.dot(h, wq_ref[:, w + j * nb:w + (j + 1) * nb], preferred_element_type=F32)
        sg_ref[0, :, cols] = (gj * _sigmoid(gj)).astype(BF16)
        kj = jnp.dot(hkv_hi, wkv_ref[:, cols], preferred_element_type=F32)
        vj = jnp.dot(hkv_hi, wkv_ref[:, w + j * nb:w + (j + 1) * nb], preferred_element_type=F32)
        v_ref[0, :, cols] = vj.astype(BF16)
        for t in range(nb // HEAD_DIM):
            hd = j * (nb // HEAD_DIM) + t
            src = slice(t * HEAD_DIM, (t + 1) * HEAD_DIM)
            bias = slice(hd * HEAD_DIM, (hd + 1) * HEAD_DIM)
            q_ref[0, :, 2 * hd * HEAD_DIM:(2 * hd + 1) * HEAD_DIM] = head_norm(qj[:, src], q_gain)
            q_ref[0, :, (2 * hd + 1) * HEAD_DIM:(2 * hd + 2) * HEAD_DIM] = q_bias[:, bias]
            k_ref[0, :, 2 * hd * HEAD_DIM:(2 * hd + 1) * HEAD_DIM] = head_norm(kj[:, src], kg_ref[...])
            k_ref[0, :, (2 * hd + 1) * HEAD_DIM:(2 * hd + 2) * HEAD_DIM] = k_bias[:, bias]


def _fox_proj_call(x, mod, kvmod, norm_g, kv_norm_g, w_in, kv_w, kv_fb, q_norm_g, k_norm_g):
    bsz, s, d = x.shape
    w = w_in.shape[1] // 2
    n_heads = w // HEAD_DIM
    assert (N_SPLIT * n_heads) < HEAD_DIM
    tm = PROJ_TM
    nb = 512
    wq = w_in.astype(BF16)
    wkv = kv_w[:, :2 * w].astype(BF16)
    wf = kv_w[:, 2 * w:]
    wf_hi = wf.astype(BF16)
    wf_lo = (wf - wf_hi.astype(F32)).astype(BF16)
    pad = jnp.zeros((d, HEAD_DIM - n_heads), BF16)
    wf_p = jnp.concatenate([wf_hi, pad, wf_lo, pad], axis=1)
    fb_p = jnp.concatenate([kv_fb, jnp.zeros((HEAD_DIM - n_heads,), F32)]).reshape(1, HEAD_DIM)
    sel_q, sel_k = _bias_selectors(n_heads)
    kern = functools.partial(_fox_proj_kernel, tm=tm, d=d, w=w, nb=nb, n_heads=n_heads)
    tok = lambda b, i: (b, i, 0)
    per_b = lambda b, i: (b, 0, 0)
    act = jax.ShapeDtypeStruct((bsz, s, w), BF16)
    aug = jax.ShapeDtypeStruct((bsz, s, 2 * w), BF16)
    return pl.pallas_call(
        kern,
        grid=(bsz, s // tm),
        in_specs=[pl.BlockSpec((1, tm, d), tok),
                  pl.BlockSpec((1, 1, 3 * d), per_b),
                  pl.BlockSpec((1, 1, 2 * d), per_b),
                  _const_spec((1, d)), _const_spec((1, d)),
                  _const_spec((d, 2 * w)), _const_spec((d, 2 * w)),
                  _const_spec((d, 2 * HEAD_DIM)),
                  _const_spec((1, HEAD_DIM)), _const_spec((1, HEAD_DIM)), _const_spec((1, HEAD_DIM)),
                  _const_spec((HEAD_DIM, w)), _const_spec((HEAD_DIM, w))],
        out_specs=[pl.BlockSpec((1, tm, 2 * w), tok), pl.BlockSpec((1, tm, w), tok),
                   pl.BlockSpec((1, tm, 2 * w), tok), pl.BlockSpec((1, tm, w), tok)],
        out_shape=[aug, act, aug, act],
        scratch_shapes=[pltpu.VMEM((1, HEAD_DIM), F32)],
        compiler_params=pltpu.CompilerParams(
            dimension_semantics=("parallel", "arbitrary"),
            vmem_limit_bytes=V7X_VMEM_LIMIT_BYTES),
        name="fox_proj",
    )(x, mod.reshape(bsz, 1, 3 * d), kvmod.reshape(bsz, 1, 2 * d), norm_g.reshape(1, d),
      kv_norm_g.reshape(1, d), wq, wkv, wf_p, fb_p, q_norm_g.reshape(1, HEAD_DIM),
      k_norm_g.reshape(1, HEAD_DIM), sel_q, sel_k)


def _fox_attn_kernel(q_ref, k_ref, v_ref, sg_ref, o_ref, *, tq, n_par):
    i = pl.program_id(2)
    aw = 2 * HEAD_DIM
    heads = range(n_par)

    def scores(j, hh):
        start = pl.multiple_of(j * tq, tq)
        q = q_ref[0, :, hh * aw:(hh + 1) * aw]
        kb = k_ref[0, pl.ds(start, tq), hh * aw:(hh + 1) * aw]
        return lax.dot_general(q, kb, (((1,), (1,)), ((), ())), preferred_element_type=F32)

    def update(s, j, hh, m, acc):
        start = pl.multiple_of(j * tq, tq)
        vb = v_ref[0, pl.ds(start, tq), hh * HEAD_DIM:(hh + 1) * HEAD_DIM]
        v_aug = jnp.concatenate([vb, jnp.ones_like(vb)], axis=1)
        m_new = jnp.maximum(m, jnp.max(s, axis=1, keepdims=True))
        alpha = jnp.exp2(m - m_new)
        p = jnp.exp2(s - m_new).astype(BF16)
        return m_new, alpha * acc + jnp.dot(p, v_aug, preferred_element_type=F32)

    def body(j, carry):
        ss = [scores(j, hh) for hh in heads]
        return tuple(update(ss[hh], j, hh, *carry[hh]) for hh in heads)

    m0 = jnp.full((tq, 1), -jnp.inf, F32)
    acc0 = jnp.zeros((tq, aw), F32)
    carry = lax.fori_loop(0, i, body, tuple((m0, acc0) for _ in heads))
    row = lax.broadcasted_iota(jnp.int32, (tq, tq), 0)
    col = lax.broadcasted_iota(jnp.int32, (tq, tq), 1)
    ss = [jnp.where(row >= col, scores(i, hh), -jnp.inf) for hh in heads]
    for hh in heads:
        _, acc = update(ss[hh], i, hh, *carry[hh])
        o = acc[:, 0:HEAD_DIM] / acc[:, HEAD_DIM:aw]
        cols = slice(hh * HEAD_DIM, (hh + 1) * HEAD_DIM)
        o_ref[0, :, cols] = (o * sg_ref[0, :, cols].astype(F32)).astype(BF16)


def _fox_attn_call(q_aug, k_aug, v, sg):
    bsz, s, w = v.shape
    n_par = ATT_HEADS
    n_groups = w // (HEAD_DIM * n_par)
    tq = ATT_TQ
    kern = functools.partial(_fox_attn_kernel, tq=tq, n_par=n_par)
    blk = lambda b, h, i: (b, i, h)
    whole = lambda b, h, i: (b, 0, h)
    return pl.pallas_call(
        kern,
        grid=(bsz, n_groups, s // tq),
        in_specs=[pl.BlockSpec((1, tq, 2 * HEAD_DIM * n_par), blk),
                  pl.BlockSpec((1, s, 2 * HEAD_DIM * n_par), whole),
                  pl.BlockSpec((1, s, HEAD_DIM * n_par), whole),
                  pl.BlockSpec((1, tq, HEAD_DIM * n_par), blk)],
        out_specs=pl.BlockSpec((1, tq, HEAD_DIM * n_par), blk),
        out_shape=jax.ShapeDtypeStruct((bsz, s, w), BF16),
        compiler_params=pltpu.CompilerParams(
            dimension_semantics=("parallel", "parallel", "arbitrary"),
            vmem_limit_bytes=V7X_VMEM_LIMIT_BYTES),
        name="fox_attn",
    )(q_aug, k_aug, v, sg)


def _out_proj_kernel(x_ref, a_ref, w_ref, mod_ref, o_ref, *, d):
    y = jnp.dot(a_ref[0], w_ref[...], preferred_element_type=F32)
    o_ref[0] = x_ref[0] + mod_ref[0, :, 2 * d:3 * d] * y


def _out_proj_call(x, a, w_out, mod):
    bsz, s, d = x.shape
    w = a.shape[2]
    tm = OUT_TM
    tok = lambda b, i: (b, i, 0)
    return pl.pallas_call(
        functools.partial(_out_proj_kernel, d=d),
        grid=(bsz, s // tm),
        in_specs=[pl.BlockSpec((1, tm, d), tok),
                  pl.BlockSpec((1, tm, w), tok),
                  _const_spec((w, d)),
                  pl.BlockSpec((1, 1, 3 * d), lambda b, i: (b, 0, 0))],
        out_specs=pl.BlockSpec((1, tm, d), tok),
        out_shape=jax.ShapeDtypeStruct((bsz, s, d), F32),
        compiler_params=pltpu.CompilerParams(
            dimension_semantics=("parallel", "parallel"),
            vmem_limit_bytes=V7X_VMEM_LIMIT_BYTES),
        name="out_proj",
    )(x, a, w_out.astype(BF16), mod.reshape(bsz, 1, 3 * d))


def kernel(x, c, mod_w, mod_b, norm_g, a_w_in, a_lb_logits, a_onorm_g, a_w_out,
           kv_mod_w, kv_mod_b, kv_norm_g, kv_w, kv_fb, k_norm_g,
           b_w_in, b_q_norm_g, b_w_out):
    assert mod_w.shape[0] == 2 and a_w_in.shape[0] == 1 and b_w_in.shape[0] == 1
    mod0 = _mod_call(c, mod_w[0], mod_b[0])
    mod1 = _mod_call(c, mod_w[1], mod_b[1])
    kvmod = _mod_call(c, kv_mod_w, kv_mod_b)

    x1 = _hgrn2_call(x, mod0, norm_g[0], a_w_in[0], a_lb_logits, a_onorm_g[0], a_w_out[0])

    q_aug, sg, k_aug, v = _fox_proj_call(x1, mod1, kvmod, norm_g[1], kv_norm_g, b_w_in[0], kv_w,
                                         kv_fb, b_q_norm_g[0], k_norm_g)
    gated = _fox_attn_call(q_aug, k_aug, v, sg)
    return _out_proj_call(x1, gated, b_w_out[0], mod1)
```

```python
import functools

import jax
import jax.numpy as jnp
import numpy as np
from jax import lax
from jax.experimental import pallas as pl
from jax.experimental.pallas import tpu as pltpu

F32 = jnp.float32
BF16 = jnp.bfloat16

HEAD_DIM = 128
CHUNK = 64
EPS = 1e-6
LOG2E = 1.4426950408889634
V7X_VMEM_LIMIT_BYTES = 56 * 1024 * 1024

HG_TM = 512
PROJ_TM = 512
ATT_TQ = 512
ATT_HEADS = 4
OUT_TM = 512


def _sigmoid(x):
    return 0.5 * jnp.tanh(0.5 * x) + 0.5


def _const_spec(shape):
    zeros = (0,) * len(shape)
    return pl.BlockSpec(shape, lambda *_: zeros, pipeline_mode=pl.Buffered(1))


def _mod_kernel(c_ref, w_ref, b_ref, o_ref):
    c = c_ref[...]
    s = c * _sigmoid(c)
    o_ref[...] = jnp.dot(s, w_ref[...], preferred_element_type=F32,
                         precision=lax.Precision.HIGHEST) + b_ref[...]


def _mod_call(c, w, b):
    bsz, d = c.shape
    n = w.shape[1]
    tn = 1024
    return pl.pallas_call(
        _mod_kernel,
        grid=(n // tn,),
        in_specs=[pl.BlockSpec((bsz, d), lambda j: (0, 0)),
                  pl.BlockSpec((d, tn), lambda j: (0, j)),
                  pl.BlockSpec((1, tn), lambda j: (0, j))],
        out_specs=pl.BlockSpec((bsz, tn), lambda j: (0, j)),
        out_shape=jax.ShapeDtypeStruct((bsz, n), F32),
        name="mod",
    )(c, w, b.reshape(1, n))


def _chunk_cumsum(x, rows_in_chunk):
    s = 1
    while s < CHUNK:
        x = x + jnp.where(rows_in_chunk >= s, pltpu.roll(x, s, 0), 0.0)
        s *= 2
    return x


def _hgrn2_kernel(x_ref, mod_ref, ng_ref, win_ref, lbl_ref, og_ref, wout_ref,
                  o_ref, st_ref, h_ref, gated_ref, pa_ref, pb_ref,
                  qd_ref, ki_ref, ks_ref, v_ref, dl_ref, sg_ref, *, tm, d, n_pairs):
    i = pl.program_id(1)
    n_chunks = tm // CHUNK
    pw = 2 * HEAD_DIM

    @pl.when(i == 0)
    def _():
        st_ref[...] = jnp.zeros_like(st_ref)

    x = x_ref[0]
    shift = mod_ref[0, :, 0:d]
    scale = mod_ref[0, :, d:2 * d]
    gate = mod_ref[0, :, 2 * d:3 * d]
    xn = x * lax.rsqrt(jnp.mean(x * x, axis=-1, keepdims=True) + EPS) * ng_ref[...]
    h_ref[...] = (xn * (1.0 + scale) + shift).astype(BF16)

    rows_in_chunk = lax.broadcasted_iota(jnp.int32, (tm, pw), 0) % CHUNK
    tril = (lax.broadcasted_iota(jnp.int32, (CHUNK, CHUNK), 0)
            >= lax.broadcasted_iota(jnp.int32, (CHUNK, CHUNK), 1))

    def in_proj(p, dst_ref):
        dst_ref[...] = jnp.dot(h_ref[...], win_ref[p], preferred_element_type=F32)

    def gates(p, proj_ref):
        q = proj_ref[:, 0:pw]
        fz = proj_ref[:, pw:2 * pw]
        g = proj_ref[:, 3 * pw:4 * pw]

        l0 = lbl_ref[0, p]
        l1 = lbl_ref[1, p]
        mx = jnp.maximum(l0, l1)
        e0 = jnp.exp(l0 - mx)
        e1 = jnp.exp(l1 - mx)
        lb = e0 / (e0 + e1)

        sig = _sigmoid(fz)
        logf = jnp.log(lb + (1.0 - lb) * sig)
        k = (1.0 - lb) * (1.0 - sig)
        b = _chunk_cumsum(logf, rows_in_chunk)
        b3 = b.reshape(n_chunks, CHUNK, pw)
        b_last = b3[:, CHUNK - 1:CHUNK, :]
        qd_ref[...] = (q * jnp.exp(b)).astype(BF16)
        ki_ref[...] = (k * jnp.exp(-b)).astype(BF16)
        ks = k.reshape(n_chunks, CHUNK, pw) * jnp.exp(b_last - b3)
        ks_ref[...] = ks.reshape(tm, pw).astype(BF16)
        v_ref[...] = proj_ref[:, 2 * pw:3 * pw].astype(BF16)
        dl_ref[...] = jnp.exp(b_last)
        sg_ref[...] = g * _sigmoid(g)

    def chunks(p):
        normed = []
        for hh in range(2):
            sl = slice(hh * HEAD_DIM, (hh + 1) * HEAD_DIM)
            st = st_ref[2 * p + hh]
            outs = []
            for c in range(n_chunks):
                rows = slice(c * CHUNK, (c + 1) * CHUNK)
                qd_c = qd_ref[rows, sl]
                v_c = v_ref[rows, sl]
                att = lax.dot_general(qd_c, ki_ref[rows, sl], (((1,), (1,)), ((), ())),
                                      preferred_element_type=F32)
                att = jnp.where(tril, att, 0.0).astype(BF16)
                o_c = jnp.dot(att, v_c, preferred_element_type=F32)
                o_c = o_c + lax.dot_general(qd_c, st.astype(BF16), (((1,), (1,)), ((), ())),
                                            preferred_element_type=F32)
                d_st = lax.dot_general(v_c, ks_ref[rows, sl], (((0,), (0,)), ((), ())),
                                       preferred_element_type=F32)
                st = st * dl_ref[c][:, sl] + d_st
                outs.append(o_c)
            st_ref[2 * p + hh] = st
            o_h = jnp.concatenate(outs, axis=0)
            o_h = o_h * lax.rsqrt(jnp.mean(o_h * o_h, axis=-1, keepdims=True) + EPS)
            normed.append(o_h)
        o_pair = jnp.concatenate(normed, axis=1) * og_ref[p]
        gated_ref[p] = (o_pair * sg_ref[...]).astype(BF16)

    def two_pairs(t, carry):
        in_proj(2 * t + 1, pb_ref)
        gates(2 * t, pa_ref)
        chunks(2 * t)
        in_proj(2 * t + 2, pa_ref)
        gates(2 * t + 1, pb_ref)
        chunks(2 * t + 1)
        return carry

    in_proj(0, pa_ref)
    lax.fori_loop(0, n_pairs // 2 - 1, two_pairs, 0)
    in_proj(n_pairs - 1, pb_ref)
    gates(n_pairs - 2, pa_ref)
    chunks(n_pairs - 2)
    gates(n_pairs - 1, pb_ref)
    chunks(n_pairs - 1)

    y = jnp.dot(gated_ref[0], wout_ref[0], preferred_element_type=F32)
    for p in range(1, n_pairs):
        y = y + jnp.dot(gated_ref[p], wout_ref[p], preferred_element_type=F32)
    o_ref[0] = x + gate * y


def _hgrn2_call(x, mod, norm_g, w_in, lb_logits, onorm_g, w_out):
    bsz, s, d = x.shape
    w = w_out.shape[0]
    n_heads = w // HEAD_DIM
    n_pairs = n_heads // 2
    pw = 2 * HEAD_DIM
    tm = HG_TM
    w_in_p = (w_in.reshape(d, 4, n_pairs, pw).transpose(2, 0, 1, 3)
              .reshape(n_pairs, d, 4 * pw).astype(BF16))
    w_out_p = w_out.reshape(n_pairs, pw, d).astype(BF16)
    kern = functools.partial(_hgrn2_kernel, tm=tm, d=d, n_pairs=n_pairs)
    return pl.pallas_call(
        kern,
        grid=(bsz, s // tm),
        in_specs=[pl.BlockSpec((1, tm, d), lambda b, i: (b, i, 0)),
                  pl.BlockSpec((1, 1, 3 * d), lambda b, i: (b, 0, 0)),
                  _const_spec((1, d)),
                  _const_spec((n_pairs, d, 4 * pw)),
                  _const_spec((2, n_pairs, 1, pw)),
                  _const_spec((n_pairs, 1, pw)),
                  _const_spec((n_pairs, pw, d))],
        out_specs=pl.BlockSpec((1, tm, d), lambda b, i: (b, i, 0)),
        out_shape=jax.ShapeDtypeStruct((bsz, s, d), F32),
        scratch_shapes=[pltpu.VMEM((n_heads, HEAD_DIM, HEAD_DIM), F32),
                        pltpu.VMEM((tm, d), BF16),
                        pltpu.VMEM((n_pairs, tm, pw), BF16),
                        pltpu.VMEM((tm, 4 * pw), F32),
                        pltpu.VMEM((tm, 4 * pw), F32),
                        pltpu.VMEM((tm, pw), BF16),
                        pltpu.VMEM((tm, pw), BF16),
                        pltpu.VMEM((tm, pw), BF16),
                        pltpu.VMEM((tm, pw), BF16),
                        pltpu.VMEM((tm // CHUNK, 1, pw), F32),
                        pltpu.VMEM((tm, pw), F32)],
        compiler_params=pltpu.CompilerParams(
            dimension_semantics=("parallel", "arbitrary"),
            vmem_limit_bytes=V7X_VMEM_LIMIT_BYTES),
        name="hgrn2_layer",
    )(x, mod.reshape(bsz, 1, 3 * d), norm_g.reshape(1, d), w_in_p,
      lb_logits.reshape(2, n_pairs, 1, pw), onorm_g.reshape(n_pairs, 1, pw), w_out_p)


N_SPLIT = 3


def _row_cumsum(x, n_rows):
    rows = lax.broadcasted_iota(jnp.int32, x.shape, 0)
    s = 1
    while s < n_rows:
        x = x + jnp.where(rows >= s, pltpu.roll(x, s, 0), 0.0)
        s *= 2
    return x


def _bias_selectors(n_heads):
    one_row = N_SPLIT * n_heads
    sel_k = np.zeros((HEAD_DIM, n_heads * HEAD_DIM), np.float32)
    sel_q = np.zeros((HEAD_DIM, n_heads * HEAD_DIM), np.float32)
    for h in range(n_heads):
        for r in range(N_SPLIT):
            sel_k[r * n_heads + h, h * HEAD_DIM + r] = -1.0
            sel_k[one_row, h * HEAD_DIM + N_SPLIT + r] = 1.0
            sel_q[one_row, h * HEAD_DIM + r] = 1.0
            sel_q[r * n_heads + h, h * HEAD_DIM + N_SPLIT + r] = 1.0
    return jnp.asarray(sel_q, BF16), jnp.asarray(sel_k, BF16)


def _fox_proj_kernel(x_ref, mod_ref, kvmod_ref, ng_ref, kvng_ref, wq_ref, wkv_ref, wf_ref,
                     fb_ref, qg_ref, kg_ref, selq_ref, selk_ref,
                     q_ref, sg_ref, k_ref, v_ref, carry_ref, *, tm, d, w, nb, n_heads):
    i = pl.program_id(1)

    @pl.when(i == 0)
    def _():
        carry_ref[...] = jnp.zeros_like(carry_ref)

    x = x_ref[0]
    xn = x * lax.rsqrt(jnp.mean(x * x, axis=-1, keepdims=True) + EPS)
    h = ((xn * ng_ref[...]) * (1.0 + mod_ref[0, :, d:2 * d]) + mod_ref[0, :, 0:d]).astype(BF16)
    hkv = (xn * kvng_ref[...]) * (1.0 + kvmod_ref[0, :, d:2 * d]) + kvmod_ref[0, :, 0:d]
    hkv_hi = hkv.astype(BF16)
    hkv_lo = (hkv - hkv_hi.astype(F32)).astype(BF16)

    r1 = jnp.dot(hkv_hi, wf_ref[...], preferred_element_type=F32)
    r2 = jnp.dot(hkv_lo, wf_ref[:, 0:HEAD_DIM], preferred_element_type=F32)
    fl = r1[:, 0:HEAD_DIM] + r1[:, HEAD_DIM:2 * HEAD_DIM] + r2 + fb_ref[...]
    log_sig = jnp.minimum(fl, 0.0) - jnp.log(1.0 + jnp.exp(-jnp.abs(fl)))
    cum = _row_cumsum(log_sig, tm) + carry_ref[...]
    carry_ref[...] = cum[tm - 1:tm, :]

    f2 = cum * LOG2E
    lane = lax.broadcasted_iota(jnp.int32, f2.shape, 1)
    split = jnp.where(lane == N_SPLIT * n_heads, 1.0, 0.0)
    rest = f2
    for r in range(N_SPLIT):
        piece = rest.astype(BF16).astype(F32)
        rest = rest - piece
        placed = piece if r == 0 else pltpu.roll(piece, r * n_heads, 1)
        split = jnp.where((lane >= r * n_heads) & (lane < (r + 1) * n_heads), placed, split)
    split = split.astype(BF16)
    q_bias = jnp.dot(split, selq_ref[...], preferred_element_type=F32).astype(BF16)
    k_bias = jnp.dot(split, selk_ref[...], preferred_element_type=F32).astype(BF16)

    def head_norm(t, gain):
        th = t * lax.rsqrt(jnp.mean(t * t, axis=-1, keepdims=True) + EPS) * gain
        return th.astype(BF16)

    q_gain = qg_ref[...] * (HEAD_DIM ** -0.5 * LOG2E)
    for j in range(w // nb):
        cols = slice(j * nb, (j + 1) * nb)
        qj = jnp.dot(h, wq_ref[:, cols], preferred_element_type=F32)
        gj = jnp.dot(h, wq_ref[:, w + j * nb:w + (j + 1) * nb], preferred_element_type=F32)
        sg_ref[0, :, cols] = (gj * _sigmoid(gj)).astype(BF16)
        kj = jnp.dot(hkv_hi, wkv_ref[:, cols], preferred_element_type=F32)
        vj = jnp.dot(hkv_hi, wkv_ref[:, w + j * nb:w + (j + 1) * nb], preferred_element_type=F32)
        v_ref[0, :, cols] = vj.astype(BF16)
        for t in range(nb // HEAD_DIM):
            hd = j * (nb // HEAD_DIM) + t
            src = slice(t * HEAD_DIM, (t + 1) * HEAD_DIM)
            bias = slice(hd * HEAD_DIM, (hd + 1) * HEAD_DIM)
            q_ref[0, :, 2 * hd * HEAD_DIM:(2 * hd + 1) * HEAD_DIM] = head_norm(qj[:, src], q_gain)
            q_ref[0, :, (2 * hd + 1) * HEAD_DIM:(2 * hd + 2) * HEAD_DIM] = q_bias[:, bias]
            k_ref[0, :, 2 * hd * HEAD_DIM:(2 * hd + 1) * HEAD_DIM] = head_norm(kj[:, src], kg_ref[...])
            k_ref[0, :, (2 * hd + 1) * HEAD_DIM:(2 * hd + 2) * HEAD_DIM] = k_bias[:, bias]


def _fox_proj_call(x, mod, kvmod, norm_g, kv_norm_g, w_in, kv_w, kv_fb, q_norm_g, k_norm_g):
    bsz, s, d = x.shape
    w = w_in.shape[1] // 2
    n_heads = w // HEAD_DIM
    assert (N_SPLIT * n_heads) < HEAD_DIM
    tm = PROJ_TM
    nb = 512
    wq = w_in.astype(BF16)
    wkv = kv_w[:, :2 * w].astype(BF16)
    wf = kv_w[:, 2 * w:]
    wf_hi = wf.astype(BF16)
    wf_lo = (wf - wf_hi.astype(F32)).astype(BF16)
    pad = jnp.zeros((d, HEAD_DIM - n_heads), BF16)
    wf_p = jnp.concatenate([wf_hi, pad, wf_lo, pad], axis=1)
    fb_p = jnp.concatenate([kv_fb, jnp.zeros((HEAD_DIM - n_heads,), F32)]).reshape(1, HEAD_DIM)
    sel_q, sel_k = _bias_selectors(n_heads)
    kern = functools.partial(_fox_proj_kernel, tm=tm, d=d, w=w, nb=nb, n_heads=n_heads)
    tok = lambda b, i: (b, i, 0)
    per_b = lambda b, i: (b, 0, 0)
    act = jax.ShapeDtypeStruct((bsz, s, w), BF16)
    aug = jax.ShapeDtypeStruct((bsz, s, 2 * w), BF16)
    return pl.pallas_call(
        kern,
        grid=(bsz, s // tm),
        in_specs=[pl.BlockSpec((1, tm, d), tok),
                  pl.BlockSpec((1, 1, 3 * d), per_b),
                  pl.BlockSpec((1, 1, 2 * d), per_b),
                  _const_spec((1, d)), _const_spec((1, d)),
                  _const_spec((d, 2 * w)), _const_spec((d, 2 * w)),
                  _const_spec((d, 2 * HEAD_DIM)),
                  _const_spec((1, HEAD_DIM)), _const_spec((1, HEAD_DIM)), _const_spec((1, HEAD_DIM)),
                  _const_spec((HEAD_DIM, w)), _const_spec((HEAD_DIM, w))],
        out_specs=[pl.BlockSpec((1, tm, 2 * w), tok), pl.BlockSpec((1, tm, w), tok),
                   pl.BlockSpec((1, tm, 2 * w), tok), pl.BlockSpec((1, tm, w), tok)],
        out_shape=[aug, act, aug, act],
        scratch_shapes=[pltpu.VMEM((1, HEAD_DIM), F32)],
        compiler_params=pltpu.CompilerParams(
            dimension_semantics=("parallel", "arbitrary"),
            vmem_limit_bytes=V7X_VMEM_LIMIT_BYTES),
        name="fox_proj",
    )(x, mod.reshape(bsz, 1, 3 * d), kvmod.reshape(bsz, 1, 2 * d), norm_g.reshape(1, d),
      kv_norm_g.reshape(1, d), wq, wkv, wf_p, fb_p, q_norm_g.reshape(1, HEAD_DIM),
      k_norm_g.reshape(1, HEAD_DIM), sel_q, sel_k)


def _fox_attn_kernel(q_ref, k_ref, v_ref, sg_ref, o_ref, *, tq, n_par):
    i = pl.program_id(2)
    aw = 2 * HEAD_DIM
    heads = range(n_par)

    def scores(j, hh):
        start = pl.multiple_of(j * tq, tq)
        q = q_ref[0, :, hh * aw:(hh + 1) * aw]
        kb = k_ref[0, pl.ds(start, tq), hh * aw:(hh + 1) * aw]
        return lax.dot_general(q, kb, (((1,), (1,)), ((), ())), preferred_element_type=F32)

    def update(s, j, hh, m, acc):
        start = pl.multiple_of(j * tq, tq)
        vb = v_ref[0, pl.ds(start, tq), hh * HEAD_DIM:(hh + 1) * HEAD_DIM]
        v_aug = jnp.concatenate([vb, jnp.ones_like(vb)], axis=1)
        m_new = jnp.maximum(m, jnp.max(s, axis=1, keepdims=True))
        alpha = jnp.exp2(m - m_new)
        p = jnp.exp2(s - m_new).astype(BF16)
        return m_new, alpha * acc + jnp.dot(p, v_aug, preferred_element_type=F32)

    def body(j, carry):
        ss = [scores(j, hh) for hh in heads]
        return tuple(update(ss[hh], j, hh, *carry[hh]) for hh in heads)

    m0 = jnp.full((tq, 1), -jnp.inf, F32)
    acc0 = jnp.zeros((tq, aw), F32)
    carry = lax.fori_loop(0, i, body, tuple((m0, acc0) for _ in heads))
    row = lax.broadcasted_iota(jnp.int32, (tq, tq), 0)
    col = lax.broadcasted_iota(jnp.int32, (tq, tq), 1)
    ss = [jnp.where(row >= col, scores(i, hh), -jnp.inf) for hh in heads]
    for hh in heads:
        _, acc = update(ss[hh], i, hh, *carry[hh])
        o = acc[:, 0:HEAD_DIM] / acc[:, HEAD_DIM:aw]
        cols = slice(hh * HEAD_DIM, (hh + 1) * HEAD_DIM)
        o_ref[0, :, cols] = (o * sg_ref[0, :, cols].astype(F32)).astype(BF16)


def _fox_attn_call(q_aug, k_aug, v, sg):
    bsz, s, w = v.shape
    n_par = ATT_HEADS
    n_groups = w // (HEAD_DIM * n_par)
    tq = ATT_TQ
    kern = functools.partial(_fox_attn_kernel, tq=tq, n_par=n_par)
    blk = lambda b, h, i: (b, i, h)
    whole = lambda b, h, i: (b, 0, h)
    return pl.pallas_call(
        kern,
        grid=(bsz, n_groups, s // tq),
        in_specs=[pl.BlockSpec((1, tq, 2 * HEAD_DIM * n_par), blk),
                  pl.BlockSpec((1, s, 2 * HEAD_DIM * n_par), whole),
                  pl.BlockSpec((1, s, HEAD_DIM * n_par), whole),
                  pl.BlockSpec((1, tq, HEAD_DIM * n_par), blk)],
        out_specs=pl.BlockSpec((1, tq, HEAD_DIM * n_par), blk),
        out_shape=jax.ShapeDtypeStruct((bsz, s, w), BF16),
        compiler_params=pltpu.CompilerParams(
            dimension_semantics=("parallel", "parallel", "arbitrary"),
            vmem_limit_bytes=V7X_VMEM_LIMIT_BYTES),
        name="fox_attn",
    )(q_aug, k_aug, v, sg)


def _out_proj_kernel(x_ref, a_ref, w_ref, mod_ref, o_ref, *, d):
    y = jnp.dot(a_ref[0], w_ref[...], preferred_element_type=F32)
    o_ref[0] = x_ref[0] + mod_ref[0, :, 2 * d:3 * d] * y


def _out_proj_call(x, a, w_out, mod):
    bsz, s, d = x.shape
    w = a.shape[2]
    tm = OUT_TM
    tok = lambda b, i: (b, i, 0)
    return pl.pallas_call(
        functools.partial(_out_proj_kernel, d=d),
        grid=(bsz, s // tm),
        in_specs=[pl.BlockSpec((1, tm, d), tok),
                  pl.BlockSpec((1, tm, w), tok),
                  _const_spec((w, d)),
                  pl.BlockSpec((1, 1, 3 * d), lambda b, i: (b, 0, 0))],
        out_specs=pl.BlockSpec((1, tm, d), tok),
        out_shape=jax.ShapeDtypeStruct((bsz, s, d), F32),
        compiler_params=pltpu.CompilerParams(
            dimension_semantics=("parallel", "parallel"),
            vmem_limit_bytes=V7X_VMEM_LIMIT_BYTES),
        name="out_proj",
    )(x, a, w_out.astype(BF16), mod.reshape(bsz, 1, 3 * d))


def kernel(x, c, mod_w, mod_b, norm_g, a_w_in, a_lb_logits, a_onorm_g, a_w_out,
           kv_mod_w, kv_mod_b, kv_norm_g, kv_w, kv_fb, k_norm_g,
           b_w_in, b_q_norm_g, b_w_out):
    assert mod_w.shape[0] == 2 and a_w_in.shape[0] == 1 and b_w_in.shape[0] == 1
    mod0 = _mod_call(c, mod_w[0], mod_b[0])
    mod1 = _mod_call(c, mod_w[1], mod_b[1])
    kvmod = _mod_call(c, kv_mod_w, kv_mod_b)

    x1 = _hgrn2_call(x, mod0, norm_g[0], a_w_in[0], a_lb_logits, a_onorm_g[0], a_w_out[0])

    q_aug, sg, k_aug, v = _fox_proj_call(x1, mod1, kvmod, norm_g[1], kv_norm_g, b_w_in[0], kv_w,
                                         kv_fb, b_q_norm_g[0], k_norm_g)
    gated = _fox_attn_call(q_aug, k_aug, v, sg)
    return _out_proj_call(x1, gated, b_w_out[0], mod1)
```

```python
import functools

import jax
import jax.numpy as jnp
import numpy as np
from jax import lax
from jax.experimental import pallas as pl
from jax.experimental.pallas import tpu as pltpu

F32 = jnp.float32
BF16 = jnp.bfloat16

HEAD_DIM = 128
CHUNK = 64
EPS = 1e-6
LOG2E = 1.4426950408889634
V7X_VMEM_LIMIT_BYTES = 56 * 1024 * 1024

HG_TM = 512
PROJ_TM = 512
ATT_TQ = 512
ATT_HEADS = 4
OUT_TM = 512


def _sigmoid(x):
    return 0.5 * jnp.tanh(0.5 * x) + 0.5


def _const_spec(shape):
    zeros = (0,) * len(shape)
    return pl.BlockSpec(shape, lambda *_: zeros, pipeline_mode=pl.Buffered(1))


def _mod_kernel(c_ref, w_ref, b_ref, o_ref):
    c = c_ref[...]
    s = c * _sigmoid(c)
    o_ref[...] = jnp.dot(s, w_ref[...], preferred_element_type=F32,
                         precision=lax.Precision.HIGHEST) + b_ref[...]


def _mod_call(c, w, b):
    bsz, d = c.shape
    n = w.shape[1]
    tn = 1024
    return pl.pallas_call(
        _mod_kernel,
        grid=(n // tn,),
        in_specs=[pl.BlockSpec((bsz, d), lambda j: (0, 0)),
                  pl.BlockSpec((d, tn), lambda j: (0, j)),
                  pl.BlockSpec((1, tn), lambda j: (0, j))],
        out_specs=pl.BlockSpec((bsz, tn), lambda j: (0, j)),
        out_shape=jax.ShapeDtypeStruct((bsz, n), F32),
        name="mod",
    )(c, w, b.reshape(1, n))


def _chunk_cumsum(x, rows_in_chunk):
    s = 1
    while s < CHUNK:
        x = x + jnp.where(rows_in_chunk >= s, pltpu.roll(x, s, 0), 0.0)
        s *= 2
    return x


def _hgrn2_kernel(x_ref, mod_ref, ng_ref, win_ref, lbl_ref, og_ref, wout_ref,
                  o_ref, st_ref, h_ref, gated_ref, pa_ref, pb_ref,
                  qd_ref, ki_ref, ks_ref, v_ref, dl_ref, sg_ref, *, tm, d, n_pairs):
    i = pl.program_id(1)
    n_chunks = tm // CHUNK
    pw = 2 * HEAD_DIM

    @pl.when(i == 0)
    def _():
        st_ref[...] = jnp.zeros_like(st_ref)

    x = x_ref[0]
    shift = mod_ref[0, :, 0:d]
    scale = mod_ref[0, :, d:2 * d]
    gate = mod_ref[0, :, 2 * d:3 * d]
    xn = x * lax.rsqrt(jnp.mean(x * x, axis=-1, keepdims=True) + EPS) * ng_ref[...]
    h_ref[...] = (xn * (1.0 + scale) + shift).astype(BF16)

    rows_in_chunk = lax.broadcasted_iota(jnp.int32, (tm, pw), 0) % CHUNK
    tril = (lax.broadcasted_iota(jnp.int32, (CHUNK, CHUNK), 0)
            >= lax.broadcasted_iota(jnp.int32, (CHUNK, CHUNK), 1))

    def in_proj(p, dst_ref):
        dst_ref[...] = jnp.dot(h_ref[...], win_ref[p], preferred_element_type=F32)

    def gates(p, proj_ref):
        q = proj_ref[:, 0:pw]
        fz = proj_ref[:, pw:2 * pw]
        g = proj_ref[:, 3 * pw:4 * pw]

        l0 = lbl_ref[0, p]
        l1 = lbl_ref[1, p]
        mx = jnp.maximum(l0, l1)
        e0 = jnp.exp(l0 - mx)
        e1 = jnp.exp(l1 - mx)
        lb = e0 / (e0 + e1)

        sig = _sigmoid(fz)
        logf = jnp.log(lb + (1.0 - lb) * sig)
        k = (1.0 - lb) * (1.0 - sig)
        b = _chunk_cumsum(logf, rows_in_chunk)
        b3 = b.reshape(n_chunks, CHUNK, pw)
        b_last = b3[:, CHUNK - 1:CHUNK, :]
        qd_ref[...] = (q * jnp.exp(b)).astype(BF16)
        ki_ref[...] = (k * jnp.exp(-b)).astype(BF16)
        ks = k.reshape(n_chunks, CHUNK, pw) * jnp.exp(b_last - b3)
        ks_ref[...] = ks.reshape(tm, pw).astype(BF16)
        v_ref[...] = proj_ref[:, 2 * pw:3 * pw].astype(BF16)
        dl_ref[...] = jnp.exp(b_last)
        sg_ref[...] = g * _sigmoid(g)

    def chunks(p):
        normed = []
        for hh in range(2):
            sl = slice(hh * HEAD_DIM, (hh + 1) * HEAD_DIM)
            st = st_ref[2 * p + hh]
            outs = []
            for c in range(n_chunks):
                rows = slice(c * CHUNK, (c + 1) * CHUNK)
                qd_c = qd_ref[rows, sl]
                v_c = v_ref[rows, sl]
                att = lax.dot_general(qd_c, ki_ref[rows, sl], (((1,), (1,)), ((), ())),
                                      preferred_element_type=F32)
                att = jnp.where(tril, att, 0.0).astype(BF16)
                o_c = jnp.dot(att, v_c, preferred_element_type=F32)
                o_c = o_c + lax.dot_general(qd_c, st.astype(BF16), (((1,), (1,)), ((), ())),
                                            preferred_element_type=F32)
                d_st = lax.dot_general(v_c, ks_ref[rows, sl], (((0,), (0,)), ((), ())),
                                       preferred_element_type=F32)
                st = st * dl_ref[c][:, sl] + d_st
                outs.append(o_c)
            st_ref[2 * p + hh] = st
            o_h = jnp.concatenate(outs, axis=0)
            o_h = o_h * lax.rsqrt(jnp.mean(o_h * o_h, axis=-1, keepdims=True) + EPS)
            normed.append(o_h)
        o_pair = jnp.concatenate(normed, axis=1) * og_ref[p]
        gated_ref[p] = (o_pair * sg_ref[...]).astype(BF16)

    def two_pairs(t, carry):
        in_proj(2 * t + 1, pb_ref)
        gates(2 * t, pa_ref)
        chunks(2 * t)
        in_proj(2 * t + 2, pa_ref)
        gates(2 * t + 1, pb_ref)
        chunks(2 * t + 1)
        return carry

    in_proj(0, pa_ref)
    lax.fori_loop(0, n_pairs // 2 - 1, two_pairs, 0)
    in_proj(n_pairs - 1, pb_ref)
    gates(n_pairs - 2, pa_ref)
    chunks(n_pairs - 2)
    gates(n_pairs - 1, pb_ref)
    chunks(n_pairs - 1)

    y = jnp.dot(gated_ref[0], wout_ref[0], preferred_element_type=F32)
    for p in range(1, n_pairs):
        y = y + jnp.dot(gated_ref[p], wout_ref[p], preferred_element_type=F32)
    o_ref[0] = x + gate * y


def _hgrn2_call(x, mod, norm_g, w_in, lb_logits, onorm_g, w_out):
    bsz, s, d = x.shape
    w = w_out.shape[0]
    n_heads = w // HEAD_DIM
    n_pairs = n_heads // 2
    pw = 2 * HEAD_DIM
    tm = HG_TM
    w_in_p = (w_in.reshape(d, 4, n_pairs, pw).transpose(2, 0, 1, 3)
              .reshape(n_pairs, d, 4 * pw).astype(BF16))
    w_out_p = w_out.reshape(n_pairs, pw, d).astype(BF16)
    kern = functools.partial(_hgrn2_kernel, tm=tm, d=d, n_pairs=n_pairs)
    return pl.pallas_call(
        kern,
        grid=(bsz, s // tm),
        in_specs=[pl.BlockSpec((1, tm, d), lambda b, i: (b, i, 0)),
                  pl.BlockSpec((1, 1, 3 * d), lambda b, i: (b, 0, 0)),
                  _const_spec((1, d)),
                  _const_spec((n_pairs, d, 4 * pw)),
                  _const_spec((2, n_pairs, 1, pw)),
                  _const_spec((n_pairs, 1, pw)),
                  _const_spec((n_pairs, pw, d))],
        out_specs=pl.BlockSpec((1, tm, d), lambda b, i: (b, i, 0)),
        out_shape=jax.ShapeDtypeStruct((bsz, s, d), F32),
        scratch_shapes=[pltpu.VMEM((n_heads, HEAD_DIM, HEAD_DIM), F32),
                        pltpu.VMEM((tm, d), BF16),
                        pltpu.VMEM((n_pairs, tm, pw), BF16),
                        pltpu.VMEM((tm, 4 * pw), F32),
                        pltpu.VMEM((tm, 4 * pw), F32),
                        pltpu.VMEM((tm, pw), BF16),
                        pltpu.VMEM((tm, pw), BF16),
                        pltpu.VMEM((tm, pw), BF16),
                        pltpu.VMEM((tm, pw), BF16),
                        pltpu.VMEM((tm // CHUNK, 1, pw), F32),
                        pltpu.VMEM((tm, pw), F32)],
        compiler_params=pltpu.CompilerParams(
            dimension_semantics=("parallel", "arbitrary"),
            vmem_limit_bytes=V7X_VMEM_LIMIT_BYTES),
        name="hgrn2_layer",
    )(x, mod.reshape(bsz, 1, 3 * d), norm_g.reshape(1, d), w_in_p,
      lb_logits.reshape(2, n_pairs, 1, pw), onorm_g.reshape(n_pairs, 1, pw), w_out_p)


N_SPLIT = 3


def _row_cumsum(x, n_rows):
    rows = lax.broadcasted_iota(jnp.int32, x.shape, 0)
    s = 1
    while s < n_rows:
        x = x + jnp.where(rows >= s, pltpu.roll(x, s, 0), 0.0)
        s *= 2
    return x


def _bias_selectors(n_heads):
    one_row = N_SPLIT * n_heads
    sel_k = np.zeros((HEAD_DIM, n_heads * HEAD_DIM), np.float32)
    sel_q = np.zeros((HEAD_DIM, n_heads * HEAD_DIM), np.float32)
    for h in range(n_heads):
        for r in range(N_SPLIT):
            sel_k[r * n_heads + h, h * HEAD_DIM + r] = -1.0
            sel_k[one_row, h * HEAD_DIM + N_SPLIT + r] = 1.0
            sel_q[one_row, h * HEAD_DIM + r] = 1.0
            sel_q[r * n_heads + h, h * HEAD_DIM + N_SPLIT + r] = 1.0
    return jnp.asarray(sel_q, BF16), jnp.asarray(sel_k, BF16)


def _fox_proj_kernel(x_ref, mod_ref, kvmod_ref, ng_ref, kvng_ref, wq_ref, wkv_ref, wf_ref,
                     fb_ref, qg_ref, kg_ref, selq_ref, selk_ref,
                     q_ref, sg_ref, k_ref, v_ref, carry_ref, *, tm, d, w, nb, n_heads):
    i = pl.program_id(1)

    @pl.when(i == 0)
    def _():
        carry_ref[...] = jnp.zeros_like(carry_ref)

    x = x_ref[0]
    xn = x * lax.rsqrt(jnp.mean(x * x, axis=-1, keepdims=True) + EPS)
    h = ((xn * ng_ref[...]) * (1.0 + mod_ref[0, :, d:2 * d]) + mod_ref[0, :, 0:d]).astype(BF16)
    hkv = (xn * kvng_ref[...]) * (1.0 + kvmod_ref[0, :, d:2 * d]) + kvmod_ref[0, :, 0:d]
    hkv_hi = hkv.astype(BF16)
    hkv_lo = (hkv - hkv_hi.astype(F32)).astype(BF16)

    r1 = jnp.dot(hkv_hi, wf_ref[...], preferred_element_type=F32)
    r2 = jnp.dot(hkv_lo, wf_ref[:, 0:HEAD_DIM], preferred_element_type=F32)
    fl = r1[:, 0:HEAD_DIM] + r1[:, HEAD_DIM:2 * HEAD_DIM] + r2 + fb_ref[...]
    log_sig = jnp.minimum(fl, 0.0) - jnp.log(1.0 + jnp.exp(-jnp.abs(fl)))
    cum = _row_cumsum(log_sig, tm) + carry_ref[...]
    carry_ref[...] = cum[tm - 1:tm, :]

    f2 = cum * LOG2E
    lane = lax.broadcasted_iota(jnp.int32, f2.shape, 1)
    split = jnp.where(lane == N_SPLIT * n_heads, 1.0, 0.0)
    rest = f2
    for r in range(N_SPLIT):
        piece = rest.astype(BF16).astype(F32)
        rest = rest - piece
        placed = piece if r == 0 else pltpu.roll(piece, r * n_heads, 1)
        split = jnp.where((lane >= r * n_heads) & (lane < (r + 1) * n_heads), placed, split)
    split = split.astype(BF16)
    q_bias = jnp.dot(split, selq_ref[...], preferred_element_type=F32).astype(BF16)
    k_bias = jnp.dot(split, selk_ref[...], preferred_element_type=F32).astype(BF16)

    def head_norm(t, gain):
        th = t * lax.rsqrt(jnp.mean(t * t, axis=-1, keepdims=True) + EPS) * gain
        return th.astype(BF16)

    q_gain = qg_ref[...] * (HEAD_DIM ** -0.5 * LOG2E)
    for j in range(w // nb):
        cols = slice(j * nb, (j + 1) * nb)
        qj = jnp.dot(h, wq_ref[:, cols], preferred_element_type=F32)
        gj = jnp.dot(h, wq_ref[:, w + j * nb:w + (j + 1) * nb], preferred_element_type=F32)
        sg_ref[0, :, cols] = (gj * _sigmoid(gj)).astype(BF16)
        kj = jnp.dot(hkv_hi, wkv_ref[:, cols], preferred_element_type=F32)
        vj = jnp.dot(hkv_hi, wkv_ref[:, w + j * nb:w + (j + 1) * nb], preferred_element_type=F32)
        v_ref[0, :, cols] = vj.astype(BF16)
        for t in range(nb // HEAD_DIM):
            hd = j * (nb // HEAD_DIM) + t
            src = slice(t * HEAD_DIM, (t + 1) * HEAD_DIM)
            bias = slice(hd * HEAD_DIM, (hd + 1) * HEAD_DIM)
            q_ref[0, :, 2 * hd * HEAD_DIM:(2 * hd + 1) * HEAD_DIM] = head_norm(qj[:, src], q_gain)
            q_ref[0, :, (2 * hd + 1) * HEAD_DIM:(2 * hd + 2) * HEAD_DIM] = q_bias[:, bias]
            k_ref[0, :, 2 * hd * HEAD_DIM:(2 * hd + 1) * HEAD_DIM] = head_norm(kj[:, src], kg_ref[...])
            k_ref[0, :, (2 * hd + 1) * HEAD_DIM:(2 * hd + 2) * HEAD_DIM] = k_bias[:, bias]


def _fox_proj_call(x, mod, kvmod, norm_g, kv_norm_g, w_in, kv_w, kv_fb, q_norm_g, k_norm_g):
    bsz, s, d = x.shape
    w = w_in.shape[1] // 2
    n_heads = w // HEAD_DIM
    assert (N_SPLIT * n_heads) < HEAD_DIM
    tm = PROJ_TM
    nb = 512
    wq = w_in.astype(BF16)
    wkv = kv_w[:, :2 * w].astype(BF16)
    wf = kv_w[:, 2 * w:]
    wf_hi = wf.astype(BF16)
    wf_lo = (wf - wf_hi.astype(F32)).astype(BF16)
    pad = jnp.zeros((d, HEAD_DIM - n_heads), BF16)
    wf_p = jnp.concatenate([wf_hi, pad, wf_lo, pad], axis=1)
    fb_p = jnp.concatenate([kv_fb, jnp.zeros((HEAD_DIM - n_heads,), F32)]).reshape(1, HEAD_DIM)
    sel_q, sel_k = _bias_selectors(n_heads)
    kern = functools.partial(_fox_proj_kernel, tm=tm, d=d, w=w, nb=nb, n_heads=n_heads)
    tok = lambda b, i: (b, i, 0)
    per_b = lambda b, i: (b, 0, 0)
    act = jax.ShapeDtypeStruct((bsz, s, w), BF16)
    aug = jax.ShapeDtypeStruct((bsz, s, 2 * w), BF16)
    return pl.pallas_call(
        kern,
        grid=(bsz, s // tm),
        in_specs=[pl.BlockSpec((1, tm, d), tok),
                  pl.BlockSpec((1, 1, 3 * d), per_b),
                  pl.BlockSpec((1, 1, 2 * d), per_b),
                  _const_spec((1, d)), _const_spec((1, d)),
                  _const_spec((d, 2 * w)), _const_spec((d, 2 * w)),
                  _const_spec((d, 2 * HEAD_DIM)),
                  _const_spec((1, HEAD_DIM)), _const_spec((1, HEAD_DIM)), _const_spec((1, HEAD_DIM)),
                  _const_spec((HEAD_DIM, w)), _const_spec((HEAD_DIM, w))],
        out_specs=[pl.BlockSpec((1, tm, 2 * w), tok), pl.BlockSpec((1, tm, w), tok),
                   pl.BlockSpec((1, tm, 2 * w), tok), pl.BlockSpec((1, tm, w), tok)],
        out_shape=[aug, act, aug, act],
        scratch_shapes=[pltpu.VMEM((1, HEAD_DIM), F32)],
        compiler_params=pltpu.CompilerParams(
            dimension_semantics=("parallel", "arbitrary"),
            vmem_limit_bytes=V7X_VMEM_LIMIT_BYTES),
        name="fox_proj",
    )(x, mod.reshape(bsz, 1, 3 * d), kvmod.reshape(bsz, 1, 2 * d), norm_g.reshape(1, d),
      kv_norm_g.reshape(1, d), wq, wkv, wf_p, fb_p, q_norm_g.reshape(1, HEAD_DIM),
      k_norm_g.reshape(1, HEAD_DIM), sel_q, sel_k)


def _fox_attn_kernel(q_ref, k_ref, v_ref, sg_ref, o_ref, acc_ref, *, tq, n_par):
    i = pl.program_id(2)
    aw = 2 * HEAD_DIM
    heads = range(n_par)
    acc_ref[...] = jnp.zeros(acc_ref.shape, F32)

    def scores(j, hh):
        start = pl.multiple_of(j * tq, tq)
        q = q_ref[0, :, hh * aw:(hh + 1) * aw]
        kb = k_ref[0, pl.ds(start, tq), hh * aw:(hh + 1) * aw]
        return lax.dot_general(q, kb, (((1,), (1,)), ((), ())), preferred_element_type=F32)

    def update(s, j, hh, m):
        start = pl.multiple_of(j * tq, tq)
        vb = v_ref[0, pl.ds(start, tq), hh * HEAD_DIM:(hh + 1) * HEAD_DIM]
        v_aug = jnp.concatenate([vb, jnp.ones_like(vb)], axis=1)
        m_new = jnp.maximum(m, jnp.max(s, axis=1, keepdims=True))
        alpha = jnp.exp2(m - m_new)
        p = jnp.exp2(s - m_new).astype(BF16)
        acc_ref[hh] = alpha * acc_ref[hh] + jnp.dot(p, v_aug, preferred_element_type=F32)
        return m_new

    def body(j, ms):
        ss = [scores(j, hh) for hh in heads]
        return tuple(update(ss[hh], j, hh, ms[hh]) for hh in heads)

    m0 = jnp.full((tq, 1), -jnp.inf, F32)
    ms = lax.fori_loop(0, i, body, tuple(m0 for _ in heads))
    row = lax.broadcasted_iota(jnp.int32, (tq, tq), 0)
    col = lax.broadcasted_iota(jnp.int32, (tq, tq), 1)
    ss = [jnp.where(row >= col, scores(i, hh), -jnp.inf) for hh in heads]
    for hh in heads:
        update(ss[hh], i, hh, ms[hh])
        acc = acc_ref[hh]
        o = acc[:, 0:HEAD_DIM] / acc[:, HEAD_DIM:aw]
        cols = slice(hh * HEAD_DIM, (hh + 1) * HEAD_DIM)
        o_ref[0, :, cols] = (o * sg_ref[0, :, cols].astype(F32)).astype(BF16)


def _fox_attn_call(q_aug, k_aug, v, sg):
    bsz, s, w = v.shape
    n_par = ATT_HEADS
    n_groups = w // (HEAD_DIM * n_par)
    tq = ATT_TQ
    kern = functools.partial(_fox_attn_kernel, tq=tq, n_par=n_par)
    blk = lambda b, h, i: (b, i, h)
    whole = lambda b, h, i: (b, 0, h)
    return pl.pallas_call(
        kern,
        grid=(bsz, n_groups, s // tq),
        in_specs=[pl.BlockSpec((1, tq, 2 * HEAD_DIM * n_par), blk),
                  pl.BlockSpec((1, s, 2 * HEAD_DIM * n_par), whole),
                  pl.BlockSpec((1, s, HEAD_DIM * n_par), whole),
                  pl.BlockSpec((1, tq, HEAD_DIM * n_par), blk)],
        out_specs=pl.BlockSpec((1, tq, HEAD_DIM * n_par), blk),
        out_shape=jax.ShapeDtypeStruct((bsz, s, w), BF16),
        scratch_shapes=[pltpu.VMEM((n_par, tq, 2 * HEAD_DIM), F32)],
        compiler_params=pltpu.CompilerParams(
            dimension_semantics=("parallel", "parallel", "arbitrary"),
            vmem_limit_bytes=V7X_VMEM_LIMIT_BYTES),
        name="fox_attn",
    )(q_aug, k_aug, v, sg)


def _out_proj_kernel(x_ref, a_ref, w_ref, mod_ref, o_ref, *, d):
    y = jnp.dot(a_ref[0], w_ref[...], preferred_element_type=F32)
    o_ref[0] = x_ref[0] + mod_ref[0, :, 2 * d:3 * d] * y


def _out_proj_call(x, a, w_out, mod):
    bsz, s, d = x.shape
    w = a.shape[2]
    tm = OUT_TM
    tok = lambda b, i: (b, i, 0)
    return pl.pallas_call(
        functools.partial(_out_proj_kernel, d=d),
        grid=(bsz, s // tm),
        in_specs=[pl.BlockSpec((1, tm, d), tok),
                  pl.BlockSpec((1, tm, w), tok),
                  _const_spec((w, d)),
                  pl.BlockSpec((1, 1, 3 * d), lambda b, i: (b, 0, 0))],
        out_specs=pl.BlockSpec((1, tm, d), tok),
        out_shape=jax.ShapeDtypeStruct((bsz, s, d), F32),
        compiler_params=pltpu.CompilerParams(
            dimension_semantics=("parallel", "parallel"),
            vmem_limit_bytes=V7X_VMEM_LIMIT_BYTES),
        name="out_proj",
    )(x, a, w_out.astype(BF16), mod.reshape(bsz, 1, 3 * d))


def kernel(x, c, mod_w, mod_b, norm_g, a_w_in, a_lb_logits, a_onorm_g, a_w_out,
           kv_mod_w, kv_mod_b, kv_norm_g, kv_w, kv_fb, k_norm_g,
           b_w_in, b_q_norm_g, b_w_out):
    assert mod_w.shape[0] == 2 and a_w_in.shape[0] == 1 and b_w_in.shape[0] == 1
    mod0 = _mod_call(c, mod_w[0], mod_b[0])
    mod1 = _mod_call(c, mod_w[1], mod_b[1])
    kvmod = _mod_call(c, kv_mod_w, kv_mod_b)

    x1 = _hgrn2_call(x, mod0, norm_g[0], a_w_in[0], a_lb_logits, a_onorm_g[0], a_w_out[0])

    q_aug, sg, k_aug, v = _fox_proj_call(x1, mod1, kvmod, norm_g[1], kv_norm_g, b_w_in[0], kv_w,
                                         kv_fb, b_q_norm_g[0], k_norm_g)
    gated = _fox_attn_call(q_aug, k_aug, v, sg)
    return _out_proj_call(x1, gated, b_w_out[0], mod1)
```

```python
import functools

import jax
import jax.numpy as jnp
import numpy as np
from jax import lax
from jax.experimental import pallas as pl
from jax.experimental.pallas import tpu as pltpu

F32 = jnp.float32
BF16 = jnp.bfloat16

HEAD_DIM = 128
CHUNK = 64
EPS = 1e-6
LOG2E = 1.4426950408889634
V7X_VMEM_LIMIT_BYTES = 56 * 1024 * 1024

HG_TM = 512
PROJ_TM = 512
ATT_TQ = 512
ATT_HEADS = 4
OUT_TM = 512


def _sigmoid(x):
    return 0.5 * jnp.tanh(0.5 * x) + 0.5


def _const_spec(shape):
    zeros = (0,) * len(shape)
    return pl.BlockSpec(shape, lambda *_: zeros, pipeline_mode=pl.Buffered(1))


def _mod_kernel(c_ref, w_ref, b_ref, o_ref):
    c = c_ref[...]
    s = c * _sigmoid(c)
    o_ref[...] = jnp.dot(s, w_ref[...], preferred_element_type=F32,
                         precision=lax.Precision.HIGHEST) + b_ref[...]


def _mod_call(c, w, b):
    bsz, d = c.shape
    n = w.shape[1]
    tn = 1024
    return pl.pallas_call(
        _mod_kernel,
        grid=(n // tn,),
        in_specs=[pl.BlockSpec((bsz, d), lambda j: (0, 0)),
                  pl.BlockSpec((d, tn), lambda j: (0, j)),
                  pl.BlockSpec((1, tn), lambda j: (0, j))],
        out_specs=pl.BlockSpec((bsz, tn), lambda j: (0, j)),
        out_shape=jax.ShapeDtypeStruct((bsz, n), F32),
        name="mod",
    )(c, w, b.reshape(1, n))


def _chunk_cumsum(x, rows_in_chunk):
    s = 1
    while s < CHUNK:
        x = x + jnp.where(rows_in_chunk >= s, pltpu.roll(x, s, 0), 0.0)
        s *= 2
    return x


def _hgrn2_kernel(x_ref, mod_ref, ng_ref, win_ref, lbl_ref, og_ref, wout_ref,
                  o_ref, st_ref, h_ref, gated_ref, pa_ref, pb_ref,
                  qd_ref, ki_ref, ks_ref, v_ref, dl_ref, sg_ref, *, tm, d, n_pairs):
    i = pl.program_id(1)
    n_chunks = tm // CHUNK
    pw = 2 * HEAD_DIM

    @pl.when(i == 0)
    def _():
        st_ref[...] = jnp.zeros_like(st_ref)

    x = x_ref[0]
    shift = mod_ref[0, :, 0:d]
    scale = mod_ref[0, :, d:2 * d]
    gate = mod_ref[0, :, 2 * d:3 * d]
    xn = x * lax.rsqrt(jnp.mean(x * x, axis=-1, keepdims=True) + EPS) * ng_ref[...]
    h_ref[...] = (xn * (1.0 + scale) + shift).astype(BF16)

    rows_in_chunk = lax.broadcasted_iota(jnp.int32, (tm, pw), 0) % CHUNK
    tril = (lax.broadcasted_iota(jnp.int32, (CHUNK, CHUNK), 0)
            >= lax.broadcasted_iota(jnp.int32, (CHUNK, CHUNK), 1))

    def in_proj(p, dst_ref):
        dst_ref[...] = jnp.dot(h_ref[...], win_ref[p], preferred_element_type=F32)

    def gates(p, proj_ref):
        q = proj_ref[:, 0:pw]
        fz = proj_ref[:, pw:2 * pw]
        g = proj_ref[:, 3 * pw:4 * pw]

        l0 = lbl_ref[0, p]
        l1 = lbl_ref[1, p]
        mx = jnp.maximum(l0, l1)
        e0 = jnp.exp(l0 - mx)
        e1 = jnp.exp(l1 - mx)
        lb = e0 / (e0 + e1)

        sig = _sigmoid(fz)
        logf = jnp.log(lb + (1.0 - lb) * sig)
        k = (1.0 - lb) * (1.0 - sig)
        b = _chunk_cumsum(logf, rows_in_chunk)
        b3 = b.reshape(n_chunks, CHUNK, pw)
        b_last = b3[:, CHUNK - 1:CHUNK, :]
        qd_ref[...] = (q * jnp.exp(b)).astype(BF16)
        ki_ref[...] = (k * jnp.exp(-b)).astype(BF16)
        ks = k.reshape(n_chunks, CHUNK, pw) * jnp.exp(b_last - b3)
        ks_ref[...] = ks.reshape(tm, pw).astype(BF16)
        v_ref[...] = proj_ref[:, 2 * pw:3 * pw].astype(BF16)
        dl_ref[...] = jnp.exp(b_last)
        sg_ref[...] = g * _sigmoid(g)

    def chunks(p):
        normed = []
        for hh in range(2):
            sl = slice(hh * HEAD_DIM, (hh + 1) * HEAD_DIM)
            st = st_ref[2 * p + hh]
            outs = []
            for c in range(n_chunks):
                rows = slice(c * CHUNK, (c + 1) * CHUNK)
                qd_c = qd_ref[rows, sl]
                v_c = v_ref[rows, sl]
                att = lax.dot_general(qd_c, ki_ref[rows, sl], (((1,), (1,)), ((), ())),
                                      preferred_element_type=F32)
                att = jnp.where(tril, att, 0.0).astype(BF16)
                o_c = jnp.dot(att, v_c, preferred_element_type=F32)
                o_c = o_c + lax.dot_general(qd_c, st.astype(BF16), (((1,), (1,)), ((), ())),
                                            preferred_element_type=F32)
                d_st = lax.dot_general(v_c, ks_ref[rows, sl], (((0,), (0,)), ((), ())),
                                       preferred_element_type=F32)
                st = st * dl_ref[c][:, sl] + d_st
                outs.append(o_c)
            st_ref[2 * p + hh] = st
            o_h = jnp.concatenate(outs, axis=0)
            o_h = o_h * lax.rsqrt(jnp.mean(o_h * o_h, axis=-1, keepdims=True) + EPS)
            normed.append(o_h)
        o_pair = jnp.concatenate(normed, axis=1) * og_ref[p]
        gated_ref[p] = (o_pair * sg_ref[...]).astype(BF16)

    def two_pairs(t, carry):
        in_proj(2 * t + 1, pb_ref)
        gates(2 * t, pa_ref)
        chunks(2 * t)
        in_proj(2 * t + 2, pa_ref)
        gates(2 * t + 1, pb_ref)
        chunks(2 * t + 1)
        return carry

    in_proj(0, pa_ref)
    lax.fori_loop(0, n_pairs // 2 - 1, two_pairs, 0)
    in_proj(n_pairs - 1, pb_ref)
    gates(n_pairs - 2, pa_ref)
    chunks(n_pairs - 2)
    gates(n_pairs - 1, pb_ref)
    chunks(n_pairs - 1)

    y = jnp.dot(gated_ref[0], wout_ref[0], preferred_element_type=F32)
    for p in range(1, n_pairs):
        y = y + jnp.dot(gated_ref[p], wout_ref[p], preferred_element_type=F32)
    o_ref[0] = x + gate * y


def _hgrn2_call(x, mod, norm_g, w_in, lb_logits, onorm_g, w_out):
    bsz, s, d = x.shape
    w = w_out.shape[0]
    n_heads = w // HEAD_DIM
    n_pairs = n_heads // 2
    pw = 2 * HEAD_DIM
    tm = HG_TM
    w_in_p = (w_in.reshape(d, 4, n_pairs, pw).transpose(2, 0, 1, 3)
              .reshape(n_pairs, d, 4 * pw).astype(BF16))
    w_out_p = w_out.reshape(n_pairs, pw, d).astype(BF16)
    kern = functools.partial(_hgrn2_kernel, tm=tm, d=d, n_pairs=n_pairs)
    return pl.pallas_call(
        kern,
        grid=(bsz, s // tm),
        in_specs=[pl.BlockSpec((1, tm, d), lambda b, i: (b, i, 0)),
                  pl.BlockSpec((1, 1, 3 * d), lambda b, i: (b, 0, 0)),
                  _const_spec((1, d)),
                  _const_spec((n_pairs, d, 4 * pw)),
                  _const_spec((2, n_pairs, 1, pw)),
                  _const_spec((n_pairs, 1, pw)),
                  _const_spec((n_pairs, pw, d))],
        out_specs=pl.BlockSpec((1, tm, d), lambda b, i: (b, i, 0)),
        out_shape=jax.ShapeDtypeStruct((bsz, s, d), F32),
        scratch_shapes=[pltpu.VMEM((n_heads, HEAD_DIM, HEAD_DIM), F32),
                        pltpu.VMEM((tm, d), BF16),
                        pltpu.VMEM((n_pairs, tm, pw), BF16),
                        pltpu.VMEM((tm, 4 * pw), F32),
                        pltpu.VMEM((tm, 4 * pw), F32),
                        pltpu.VMEM((tm, pw), BF16),
                        pltpu.VMEM((tm, pw), BF16),
                        pltpu.VMEM((tm, pw), BF16),
                        pltpu.VMEM((tm, pw), BF16),
                        pltpu.VMEM((tm // CHUNK, 1, pw), F32),
                        pltpu.VMEM((tm, pw), F32)],
        compiler_params=pltpu.CompilerParams(
            dimension_semantics=("parallel", "arbitrary"),
            vmem_limit_bytes=V7X_VMEM_LIMIT_BYTES),
        name="hgrn2_layer",
    )(x, mod.reshape(bsz, 1, 3 * d), norm_g.reshape(1, d), w_in_p,
      lb_logits.reshape(2, n_pairs, 1, pw), onorm_g.reshape(n_pairs, 1, pw), w_out_p)


N_SPLIT = 3


def _row_cumsum(x, n_rows):
    rows = lax.broadcasted_iota(jnp.int32, x.shape, 0)
    s = 1
    while s < n_rows:
        x = x + jnp.where(rows >= s, pltpu.roll(x, s, 0), 0.0)
        s *= 2
    return x


def _bias_selectors(n_heads):
    one_row = N_SPLIT * n_heads
    sel_k = np.zeros((HEAD_DIM, n_heads * HEAD_DIM), np.float32)
    sel_q = np.zeros((HEAD_DIM, n_heads * HEAD_DIM), np.float32)
    for h in range(n_heads):
        for r in range(N_SPLIT):
            sel_k[r * n_heads + h, h * HEAD_DIM + r] = -1.0
            sel_k[one_row, h * HEAD_DIM + N_SPLIT + r] = 1.0
            sel_q[one_row, h * HEAD_DIM + r] = 1.0
            sel_q[r * n_heads + h, h * HEAD_DIM + N_SPLIT + r] = 1.0
    return jnp.asarray(sel_q, BF16), jnp.asarray(sel_k, BF16)


def _fox_proj_kernel(x_ref, mod_ref, kvmod_ref, ng_ref, kvng_ref, wq_ref, wkv_ref, wf_ref,
                     fb_ref, qg_ref, kg_ref, selq_ref, selk_ref,
                     q_ref, sg_ref, k_ref, v_ref, carry_ref, *, tm, d, w, nb, n_heads):
    i = pl.program_id(1)

    @pl.when(i == 0)
    def _():
        carry_ref[...] = jnp.zeros_like(carry_ref)

    x = x_ref[0]
    xn = x * lax.rsqrt(jnp.mean(x * x, axis=-1, keepdims=True) + EPS)
    h = ((xn * ng_ref[...]) * (1.0 + mod_ref[0, :, d:2 * d]) + mod_ref[0, :, 0:d]).astype(BF16)
    hkv = (xn * kvng_ref[...]) * (1.0 + kvmod_ref[0, :, d:2 * d]) + kvmod_ref[0, :, 0:d]
    hkv_hi = hkv.astype(BF16)
    hkv_lo = (hkv - hkv_hi.astype(F32)).astype(BF16)

    r1 = jnp.dot(hkv_hi, wf_ref[...], preferred_element_type=F32)
    r2 = jnp.dot(hkv_lo, wf_ref[:, 0:HEAD_DIM], preferred_element_type=F32)
    fl = r1[:, 0:HEAD_DIM] + r1[:, HEAD_DIM:2 * HEAD_DIM] + r2 + fb_ref[...]
    log_sig = jnp.minimum(fl, 0.0) - jnp.log(1.0 + jnp.exp(-jnp.abs(fl)))
    cum = _row_cumsum(log_sig, tm) + carry_ref[...]
    carry_ref[...] = cum[tm - 1:tm, :]

    f2 = cum * LOG2E
    lane = lax.broadcasted_iota(jnp.int32, f2.shape, 1)
    split = jnp.where(lane == N_SPLIT * n_heads, 1.0, 0.0)
    rest = f2
    for r in range(N_SPLIT):
        piece = rest.astype(BF16).astype(F32)
        rest = rest - piece
        placed = piece if r == 0 else pltpu.roll(piece, r * n_heads, 1)
        split = jnp.where((lane >= r * n_heads) & (lane < (r + 1) * n_heads), placed, split)
    split = split.astype(BF16)
    q_bias = jnp.dot(split, selq_ref[...], preferred_element_type=F32).astype(BF16)
    k_bias_f32 = jnp.dot(split, selk_ref[...], preferred_element_type=F32)

    def head_norm(t, gain):
        th = t * lax.rsqrt(jnp.mean(t * t, axis=-1, keepdims=True) + EPS) * gain
        return th.astype(BF16)

    q_gain = qg_ref[...] * (HEAD_DIM ** -0.5 * LOG2E)
    for j in range(w // nb):
        cols = slice(j * nb, (j + 1) * nb)
        qj = jnp.dot(h, wq_ref[:, cols], preferred_element_type=F32)
        gj = jnp.dot(h, wq_ref[:, w + j * nb:w + (j + 1) * nb], preferred_element_type=F32)
        sg_ref[0, :, cols] = (gj * _sigmoid(gj)).astype(BF16)
        kj = jnp.dot(hkv_hi, wkv_ref[:, cols], preferred_element_type=F32)
        vj = jnp.dot(hkv_hi, wkv_ref[:, w + j * nb:w + (j + 1) * nb], preferred_element_type=F32)
        v_ref[0, :, cols] = vj.astype(BF16)
        for t in range(nb // HEAD_DIM):
            hd = j * (nb // HEAD_DIM) + t
            src = slice(t * HEAD_DIM, (t + 1) * HEAD_DIM)
            bias = slice(hd * HEAD_DIM, (hd + 1) * HEAD_DIM)
            q_ref[0, :, 2 * hd * HEAD_DIM:(2 * hd + 1) * HEAD_DIM] = head_norm(qj[:, src], q_gain)
            q_ref[0, :, (2 * hd + 1) * HEAD_DIM:(2 * hd + 2) * HEAD_DIM] = q_bias[:, bias]
            kn = kj[:, src]
            kn = kn * lax.rsqrt(jnp.mean(kn * kn, axis=-1, keepdims=True) + EPS) * kg_ref[...]
            k_ref[0, hd, 0, 0:HEAD_DIM, :] = kn.T.astype(BF16)
            k_ref[0, hd, 0, HEAD_DIM:2 * HEAD_DIM, :] = k_bias_f32[:, bias].T.astype(BF16)


def _fox_proj_call(x, mod, kvmod, norm_g, kv_norm_g, w_in, kv_w, kv_fb, q_norm_g, k_norm_g):
    bsz, s, d = x.shape
    w = w_in.shape[1] // 2
    n_heads = w // HEAD_DIM
    assert (N_SPLIT * n_heads) < HEAD_DIM
    tm = PROJ_TM
    nb = 512
    wq = w_in.astype(BF16)
    wkv = kv_w[:, :2 * w].astype(BF16)
    wf = kv_w[:, 2 * w:]
    wf_hi = wf.astype(BF16)
    wf_lo = (wf - wf_hi.astype(F32)).astype(BF16)
    pad = jnp.zeros((d, HEAD_DIM - n_heads), BF16)
    wf_p = jnp.concatenate([wf_hi, pad, wf_lo, pad], axis=1)
    fb_p = jnp.concatenate([kv_fb, jnp.zeros((HEAD_DIM - n_heads,), F32)]).reshape(1, HEAD_DIM)
    sel_q, sel_k = _bias_selectors(n_heads)
    kern = functools.partial(_fox_proj_kernel, tm=tm, d=d, w=w, nb=nb, n_heads=n_heads)
    tok = lambda b, i: (b, i, 0)
    per_b = lambda b, i: (b, 0, 0)
    act = jax.ShapeDtypeStruct((bsz, s, w), BF16)
    aug = jax.ShapeDtypeStruct((bsz, s, 2 * w), BF16)
    kt_aug = jax.ShapeDtypeStruct((bsz, n_heads, s // tm, 2 * HEAD_DIM, tm), BF16)
    return pl.pallas_call(
        kern,
        grid=(bsz, s // tm),
        in_specs=[pl.BlockSpec((1, tm, d), tok),
                  pl.BlockSpec((1, 1, 3 * d), per_b),
                  pl.BlockSpec((1, 1, 2 * d), per_b),
                  _const_spec((1, d)), _const_spec((1, d)),
                  _const_spec((d, 2 * w)), _const_spec((d, 2 * w)),
                  _const_spec((d, 2 * HEAD_DIM)),
                  _const_spec((1, HEAD_DIM)), _const_spec((1, HEAD_DIM)), _const_spec((1, HEAD_DIM)),
                  _const_spec((HEAD_DIM, w)), _const_spec((HEAD_DIM, w))],
        out_specs=[pl.BlockSpec((1, tm, 2 * w), tok), pl.BlockSpec((1, tm, w), tok),
                   pl.BlockSpec((1, n_heads, 1, 2 * HEAD_DIM, tm), lambda b, i: (b, 0, i, 0, 0)),
                   pl.BlockSpec((1, tm, w), tok)],
        out_shape=[aug, act, kt_aug, act],
        scratch_shapes=[pltpu.VMEM((1, HEAD_DIM), F32)],
        compiler_params=pltpu.CompilerParams(
            dimension_semantics=("parallel", "arbitrary"),
            vmem_limit_bytes=V7X_VMEM_LIMIT_BYTES),
        name="fox_proj",
    )(x, mod.reshape(bsz, 1, 3 * d), kvmod.reshape(bsz, 1, 2 * d), norm_g.reshape(1, d),
      kv_norm_g.reshape(1, d), wq, wkv, wf_p, fb_p, q_norm_g.reshape(1, HEAD_DIM),
      k_norm_g.reshape(1, HEAD_DIM), sel_q, sel_k)


def _fox_attn_kernel(q_ref, k_ref, v_ref, sg_ref, o_ref, acc_ref, *, tq, n_par):
    i = pl.program_id(2)
    aw = 2 * HEAD_DIM
    heads = range(n_par)
    acc_ref[...] = jnp.zeros(acc_ref.shape, F32)

    def scores(hh, r0, nr, kblk, nc):
        q = q_ref[0, r0:r0 + nr, hh * aw:(hh + 1) * aw]
        kt = k_ref[0, hh, kblk, :, 0:nc]
        return jnp.dot(q, kt, preferred_element_type=F32)

    def update(s, hh, m, r0, nr, c0, nc):
        vb = v_ref[0, pl.ds(c0, nc), hh * HEAD_DIM:(hh + 1) * HEAD_DIM]
        v_aug = jnp.concatenate([vb, jnp.ones_like(vb)], axis=1)
        m_new = jnp.maximum(m, jnp.max(s, axis=1, keepdims=True))
        alpha = jnp.exp2(m - m_new)
        p = jnp.exp2(s - m_new).astype(BF16)
        acc_ref[hh, r0:r0 + nr, :] = (alpha * acc_ref[hh, r0:r0 + nr, :]
                                      + jnp.dot(p, v_aug, preferred_element_type=F32))
        return m_new

    def body(j, ms):
        c0 = pl.multiple_of(j * tq, tq)
        ss = [scores(hh, 0, tq, j, tq) for hh in heads]
        return tuple(update(ss[hh], hh, ms[hh], 0, tq, c0, tq) for hh in heads)

    m0 = jnp.full((tq, 1), -jnp.inf, F32)
    ms = lax.fori_loop(0, i, body, tuple(m0 for _ in heads))

    half = tq // 2
    d0 = pl.multiple_of(i * tq, tq)
    top_mask = (lax.broadcasted_iota(jnp.int32, (half, half), 0)
                >= lax.broadcasted_iota(jnp.int32, (half, half), 1))
    bot_mask = (lax.broadcasted_iota(jnp.int32, (half, tq), 0) + half
                >= lax.broadcasted_iota(jnp.int32, (half, tq), 1))
    s_top = [jnp.where(top_mask, scores(hh, 0, half, i, half), -jnp.inf) for hh in heads]
    s_bot = [jnp.where(bot_mask, scores(hh, half, half, i, tq), -jnp.inf) for hh in heads]
    for hh in heads:
        update(s_top[hh], hh, ms[hh][0:half], 0, half, d0, half)
        update(s_bot[hh], hh, ms[hh][half:tq], half, half, d0, tq)
        acc = acc_ref[hh]
        o = acc[:, 0:HEAD_DIM] / acc[:, HEAD_DIM:aw]
        cols = slice(hh * HEAD_DIM, (hh + 1) * HEAD_DIM)
        o_ref[0, :, cols] = (o * sg_ref[0, :, cols].astype(F32)).astype(BF16)


def _fox_attn_call(q_aug, k_aug, v, sg):
    bsz, s, w = v.shape
    n_par = ATT_HEADS
    n_groups = w // (HEAD_DIM * n_par)
    tq = ATT_TQ
    assert k_aug.shape == (bsz, w // HEAD_DIM, s // tq, 2 * HEAD_DIM, tq)
    kern = functools.partial(_fox_attn_kernel, tq=tq, n_par=n_par)
    blk = lambda b, h, i: (b, i, h)
    whole = lambda b, h, i: (b, 0, h)
    return pl.pallas_call(
        kern,
        grid=(bsz, n_groups, s // tq),
        in_specs=[pl.BlockSpec((1, tq, 2 * HEAD_DIM * n_par), blk),
                  pl.BlockSpec((1, n_par, s // tq, 2 * HEAD_DIM, tq), lambda b, h, i: (b, h, 0, 0, 0)),
                  pl.BlockSpec((1, s, HEAD_DIM * n_par), whole),
                  pl.BlockSpec((1, tq, HEAD_DIM * n_par), blk)],
        out_specs=pl.BlockSpec((1, tq, HEAD_DIM * n_par), blk),
        out_shape=jax.ShapeDtypeStruct((bsz, s, w), BF16),
        scratch_shapes=[pltpu.VMEM((n_par, tq, 2 * HEAD_DIM), F32)],
        compiler_params=pltpu.CompilerParams(
            dimension_semantics=("parallel", "parallel", "arbitrary"),
            vmem_limit_bytes=V7X_VMEM_LIMIT_BYTES),
        name="fox_attn",
    )(q_aug, k_aug, v, sg)


def _out_proj_kernel(x_ref, a_ref, w_ref, mod_ref, o_ref, *, d):
    y = jnp.dot(a_ref[0], w_ref[...], preferred_element_type=F32)
    o_ref[0] = x_ref[0] + mod_ref[0, :, 2 * d:3 * d] * y


def _out_proj_call(x, a, w_out, mod):
    bsz, s, d = x.shape
    w = a.shape[2]
    tm = OUT_TM
    tok = lambda b, i: (b, i, 0)
    return pl.pallas_call(
        functools.partial(_out_proj_kernel, d=d),
        grid=(bsz, s // tm),
        in_specs=[pl.BlockSpec((1, tm, d), tok),
                  pl.BlockSpec((1, tm, w), tok),
                  _const_spec((w, d)),
                  pl.BlockSpec((1, 1, 3 * d), lambda b, i: (b, 0, 0))],
        out_specs=pl.BlockSpec((1, tm, d), tok),
        out_shape=jax.ShapeDtypeStruct((bsz, s, d), F32),
        compiler_params=pltpu.CompilerParams(
            dimension_semantics=("parallel", "parallel"),
            vmem_limit_bytes=V7X_VMEM_LIMIT_BYTES),
        name="out_proj",
    )(x, a, w_out.astype(BF16), mod.reshape(bsz, 1, 3 * d))


def kernel(x, c, mod_w, mod_b, norm_g, a_w_in, a_lb_logits, a_onorm_g, a_w_out,
           kv_mod_w, kv_mod_b, kv_norm_g, kv_w, kv_fb, k_norm_g,
           b_w_in, b_q_norm_g, b_w_out):
    assert mod_w.shape[0] == 2 and a_w_in.shape[0] == 1 and b_w_in.shape[0] == 1
    mod0 = _mod_call(c, mod_w[0], mod_b[0])
    mod1 = _mod_call(c, mod_w[1], mod_b[1])
    kvmod = _mod_call(c, kv_mod_w, kv_mod_b)

    x1 = _hgrn2_call(x, mod0, norm_g[0], a_w_in[0], a_lb_logits, a_onorm_g[0], a_w_out[0])

    q_aug, sg, k_aug, v = _fox_proj_call(x1, mod1, kvmod, norm_g[1], kv_norm_g, b_w_in[0], kv_w,
                                         kv_fb, b_q_norm_g[0], k_norm_g)
    gated = _fox_attn_call(q_aug, k_aug, v, sg)
    return _out_proj_call(x1, gated, b_w_out[0], mod1)
```

```python
import functools

import jax
import jax.numpy as jnp
import numpy as np
from jax import lax
from jax.experimental import pallas as pl
from jax.experimental.pallas import tpu as pltpu

F32 = jnp.float32
BF16 = jnp.bfloat16

HEAD_DIM = 128
CHUNK = 64
EPS = 1e-6
LOG2E = 1.4426950408889634
V7X_VMEM_LIMIT_BYTES = 56 * 1024 * 1024

HG_TM = 512
PROJ_TM = 512
ATT_TQ = 512
ATT_HEADS = 4
OUT_TM = 512


def _sigmoid(x):
    return 0.5 * jnp.tanh(0.5 * x) + 0.5


def _const_spec(shape):
    zeros = (0,) * len(shape)
    return pl.BlockSpec(shape, lambda *_: zeros, pipeline_mode=pl.Buffered(1))


def _mod_kernel(c_ref, w_ref, b_ref, o_ref):
    c = c_ref[...]
    s = c * _sigmoid(c)
    o_ref[...] = jnp.dot(s, w_ref[...], preferred_element_type=F32,
                         precision=lax.Precision.HIGHEST) + b_ref[...]


def _mod_call(c, w, b):
    bsz, d = c.shape
    n = w.shape[1]
    tn = 1024
    return pl.pallas_call(
        _mod_kernel,
        grid=(n // tn,),
        in_specs=[pl.BlockSpec((bsz, d), lambda j: (0, 0)),
                  pl.BlockSpec((d, tn), lambda j: (0, j)),
                  pl.BlockSpec((1, tn), lambda j: (0, j))],
        out_specs=pl.BlockSpec((bsz, tn), lambda j: (0, j)),
        out_shape=jax.ShapeDtypeStruct((bsz, n), F32),
        name="mod",
    )(c, w, b.reshape(1, n))


def _chunk_cumsum(x, rows_in_chunk):
    s = 1
    while s < CHUNK:
        x = x + jnp.where(rows_in_chunk >= s, pltpu.roll(x, s, 0), 0.0)
        s *= 2
    return x


def _hgrn2_kernel(x_ref, mod_ref, ng_ref, win_ref, lbl_ref, og_ref, wout_ref,
                  o_ref, st_ref, h_ref, gated_ref, pa_ref, pb_ref,
                  qd_ref, ki_ref, ks_ref, v_ref, dl_ref, sg_ref, *, tm, d, n_pairs):
    i = pl.program_id(1)
    n_chunks = tm // CHUNK
    pw = 2 * HEAD_DIM

    @pl.when(i == 0)
    def _():
        st_ref[...] = jnp.zeros_like(st_ref)

    x = x_ref[0]
    shift = mod_ref[0, :, 0:d]
    scale = mod_ref[0, :, d:2 * d]
    gate = mod_ref[0, :, 2 * d:3 * d]
    xn = x * lax.rsqrt(jnp.mean(x * x, axis=-1, keepdims=True) + EPS) * ng_ref[...]
    h_ref[...] = (xn * (1.0 + scale) + shift).astype(BF16)

    rows_in_chunk = lax.broadcasted_iota(jnp.int32, (tm, pw), 0) % CHUNK
    tril = (lax.broadcasted_iota(jnp.int32, (CHUNK, CHUNK), 0)
            >= lax.broadcasted_iota(jnp.int32, (CHUNK, CHUNK), 1))

    def in_proj(p, dst_ref):
        dst_ref[...] = jnp.dot(h_ref[...], win_ref[p], preferred_element_type=F32)

    def gates(p, proj_ref):
        q = proj_ref[:, 0:pw]
        fz = proj_ref[:, pw:2 * pw]
        g = proj_ref[:, 3 * pw:4 * pw]

        l0 = lbl_ref[0, p]
        l1 = lbl_ref[1, p]
        mx = jnp.maximum(l0, l1)
        e0 = jnp.exp(l0 - mx)
        e1 = jnp.exp(l1 - mx)
        lb = e0 / (e0 + e1)

        sig = _sigmoid(fz)
        logf = jnp.log(lb + (1.0 - lb) * sig)
        k = (1.0 - lb) * (1.0 - sig)
        b = _chunk_cumsum(logf, rows_in_chunk)
        b3 = b.reshape(n_chunks, CHUNK, pw)
        b_last = b3[:, CHUNK - 1:CHUNK, :]
        qd_ref[...] = (q * jnp.exp(b)).astype(BF16)
        ki_ref[...] = (k * jnp.exp(-b)).astype(BF16)
        ks = k.reshape(n_chunks, CHUNK, pw) * jnp.exp(b_last - b3)
        ks_ref[...] = ks.reshape(tm, pw).astype(BF16)
        v_ref[...] = proj_ref[:, 2 * pw:3 * pw].astype(BF16)
        dl_ref[...] = jnp.exp(b_last)
        sg_ref[...] = g * _sigmoid(g)

    def chunks(p):
        normed = []
        for hh in range(2):
            sl = slice(hh * HEAD_DIM, (hh + 1) * HEAD_DIM)
            st = st_ref[2 * p + hh]
            outs = []
            for c in range(n_chunks):
                rows = slice(c * CHUNK, (c + 1) * CHUNK)
                qd_c = qd_ref[rows, sl]
                v_c = v_ref[rows, sl]
                att = lax.dot_general(qd_c, ki_ref[rows, sl], (((1,), (1,)), ((), ())),
                                      preferred_element_type=F32)
                att = jnp.where(tril, att, 0.0).astype(BF16)
                o_c = jnp.dot(att, v_c, preferred_element_type=F32)
                o_c = o_c + lax.dot_general(qd_c, st.astype(BF16), (((1,), (1,)), ((), ())),
                                            preferred_element_type=F32)
                d_st = lax.dot_general(v_c, ks_ref[rows, sl], (((0,), (0,)), ((), ())),
                                       preferred_element_type=F32)
                st = st * dl_ref[c][:, sl] + d_st
                outs.append(o_c)
            st_ref[2 * p + hh] = st
            o_h = jnp.concatenate(outs, axis=0)
            o_h = o_h * lax.rsqrt(jnp.mean(o_h * o_h, axis=-1, keepdims=True) + EPS)
            normed.append(o_h)
        o_pair = jnp.concatenate(normed, axis=1) * og_ref[p]
        gated_ref[p] = (o_pair * sg_ref[...]).astype(BF16)

    def two_pairs(t, carry):
        in_proj(2 * t + 1, pb_ref)
        gates(2 * t, pa_ref)
        chunks(2 * t)
        in_proj(2 * t + 2, pa_ref)
        gates(2 * t + 1, pb_ref)
        chunks(2 * t + 1)
        return carry

    in_proj(0, pa_ref)
    lax.fori_loop(0, n_pairs // 2 - 1, two_pairs, 0)
    in_proj(n_pairs - 1, pb_ref)
    gates(n_pairs - 2, pa_ref)
    chunks(n_pairs - 2)
    gates(n_pairs - 1, pb_ref)
    chunks(n_pairs - 1)

    y = jnp.dot(gated_ref[0], wout_ref[0], preferred_element_type=F32)
    for p in range(1, n_pairs):
        y = y + jnp.dot(gated_ref[p], wout_ref[p], preferred_element_type=F32)
    o_ref[0] = x + gate * y


def _hgrn2_call(x, mod, norm_g, w_in, lb_logits, onorm_g, w_out):
    bsz, s, d = x.shape
    w = w_out.shape[0]
    n_heads = w // HEAD_DIM
    n_pairs = n_heads // 2
    pw = 2 * HEAD_DIM
    tm = HG_TM
    w_in_p = (w_in.reshape(d, 4, n_pairs, pw).transpose(2, 0, 1, 3)
              .reshape(n_pairs, d, 4 * pw).astype(BF16))
    w_out_p = w_out.reshape(n_pairs, pw, d).astype(BF16)
    kern = functools.partial(_hgrn2_kernel, tm=tm, d=d, n_pairs=n_pairs)
    return pl.pallas_call(
        kern,
        grid=(bsz, s // tm),
        in_specs=[pl.BlockSpec((1, tm, d), lambda b, i: (b, i, 0)),
                  pl.BlockSpec((1, 1, 3 * d), lambda b, i: (b, 0, 0)),
                  _const_spec((1, d)),
                  _const_spec((n_pairs, d, 4 * pw)),
                  _const_spec((2, n_pairs, 1, pw)),
                  _const_spec((n_pairs, 1, pw)),
                  _const_spec((n_pairs, pw, d))],
        out_specs=pl.BlockSpec((1, tm, d), lambda b, i: (b, i, 0)),
        out_shape=jax.ShapeDtypeStruct((bsz, s, d), F32),
        scratch_shapes=[pltpu.VMEM((n_heads, HEAD_DIM, HEAD_DIM), F32),
                        pltpu.VMEM((tm, d), BF16),
                        pltpu.VMEM((n_pairs, tm, pw), BF16),
                        pltpu.VMEM((tm, 4 * pw), F32),
                        pltpu.VMEM((tm, 4 * pw), F32),
                        pltpu.VMEM((tm, pw), BF16),
                        pltpu.VMEM((tm, pw), BF16),
                        pltpu.VMEM((tm, pw), BF16),
                        pltpu.VMEM((tm, pw), BF16),
                        pltpu.VMEM((tm // CHUNK, 1, pw), F32),
                        pltpu.VMEM((tm, pw), F32)],
        compiler_params=pltpu.CompilerParams(
            dimension_semantics=("parallel", "arbitrary"),
            vmem_limit_bytes=V7X_VMEM_LIMIT_BYTES),
        name="hgrn2_layer",
    )(x, mod.reshape(bsz, 1, 3 * d), norm_g.reshape(1, d), w_in_p,
      lb_logits.reshape(2, n_pairs, 1, pw), onorm_g.reshape(n_pairs, 1, pw), w_out_p)


N_SPLIT = 3


def _row_cumsum(x, n_rows):
    rows = lax.broadcasted_iota(jnp.int32, x.shape, 0)
    s = 1
    while s < n_rows:
        x = x + jnp.where(rows >= s, pltpu.roll(x, s, 0), 0.0)
        s *= 2
    return x


def _bias_selectors(n_heads):
    one_row = N_SPLIT * n_heads
    sel_k = np.zeros((HEAD_DIM, n_heads * HEAD_DIM), np.float32)
    sel_q = np.zeros((HEAD_DIM, n_heads * HEAD_DIM), np.float32)
    for h in range(n_heads):
        for r in range(N_SPLIT):
            sel_k[r * n_heads + h, h * HEAD_DIM + r] = -1.0
            sel_k[one_row, h * HEAD_DIM + N_SPLIT + r] = 1.0
            sel_q[one_row, h * HEAD_DIM + r] = 1.0
            sel_q[r * n_heads + h, h * HEAD_DIM + N_SPLIT + r] = 1.0
    return jnp.asarray(sel_q, BF16), jnp.asarray(sel_k, BF16)


def _fox_proj_kernel(x_ref, mod_ref, kvmod_ref, ng_ref, kvng_ref, wq_ref, wkv_ref, wf_ref,
                     fb_ref, qg_ref, kg_ref, selq_ref, selk_ref,
                     q_ref, sg_ref, k_ref, v_ref, carry_ref, *, tm, d, w, nb, n_heads):
    i = pl.program_id(1)

    @pl.when(i == 0)
    def _():
        carry_ref[...] = jnp.zeros_like(carry_ref)

    x = x_ref[0]
    xn = x * lax.rsqrt(jnp.mean(x * x, axis=-1, keepdims=True) + EPS)
    h = ((xn * ng_ref[...]) * (1.0 + mod_ref[0, :, d:2 * d]) + mod_ref[0, :, 0:d]).astype(BF16)
    hkv = (xn * kvng_ref[...]) * (1.0 + kvmod_ref[0, :, d:2 * d]) + kvmod_ref[0, :, 0:d]
    hkv_hi = hkv.astype(BF16)
    hkv_lo = (hkv - hkv_hi.astype(F32)).astype(BF16)

    r1 = jnp.dot(hkv_hi, wf_ref[...], preferred_element_type=F32)
    r2 = jnp.dot(hkv_lo, wf_ref[:, 0:HEAD_DIM], preferred_element_type=F32)
    fl = r1[:, 0:HEAD_DIM] + r1[:, HEAD_DIM:2 * HEAD_DIM] + r2 + fb_ref[...]
    log_sig = jnp.minimum(fl, 0.0) - jnp.log(1.0 + jnp.exp(-jnp.abs(fl)))
    cum = _row_cumsum(log_sig, tm) + carry_ref[...]
    carry_ref[...] = cum[tm - 1:tm, :]

    f2 = cum * LOG2E
    lane = lax.broadcasted_iota(jnp.int32, f2.shape, 1)
    split = jnp.where(lane == N_SPLIT * n_heads, 1.0, 0.0)
    rest = f2
    for r in range(N_SPLIT):
        piece = rest.astype(BF16).astype(F32)
        rest = rest - piece
        placed = piece if r == 0 else pltpu.roll(piece, r * n_heads, 1)
        split = jnp.where((lane >= r * n_heads) & (lane < (r + 1) * n_heads), placed, split)
    split = split.astype(BF16)
    q_bias = jnp.dot(split, selq_ref[...], preferred_element_type=F32).astype(BF16)
    k_bias_f32 = jnp.dot(split, selk_ref[...], preferred_element_type=F32)

    def head_norm(t, gain):
        th = t * lax.rsqrt(jnp.mean(t * t, axis=-1, keepdims=True) + EPS) * gain
        return th.astype(BF16)

    q_gain = qg_ref[...] * (HEAD_DIM ** -0.5 * LOG2E)
    for j in range(w // nb):
        cols = slice(j * nb, (j + 1) * nb)
        qj = jnp.dot(h, wq_ref[:, cols], preferred_element_type=F32)
        gj = jnp.dot(h, wq_ref[:, w + j * nb:w + (j + 1) * nb], preferred_element_type=F32)
        sg_ref[0, :, cols] = (gj * _sigmoid(gj)).astype(BF16)
        kj = jnp.dot(hkv_hi, wkv_ref[:, cols], preferred_element_type=F32)
        vj = jnp.dot(hkv_hi, wkv_ref[:, w + j * nb:w + (j + 1) * nb], preferred_element_type=F32)
        v_ref[0, :, cols] = vj.astype(BF16)
        for t in range(nb // HEAD_DIM):
            hd = j * (nb // HEAD_DIM) + t
            src = slice(t * HEAD_DIM, (t + 1) * HEAD_DIM)
            bias = slice(hd * HEAD_DIM, (hd + 1) * HEAD_DIM)
            q_ref[0, :, 2 * hd * HEAD_DIM:(2 * hd + 1) * HEAD_DIM] = head_norm(qj[:, src], q_gain)
            q_ref[0, :, (2 * hd + 1) * HEAD_DIM:(2 * hd + 2) * HEAD_DIM] = q_bias[:, bias]
            kn = kj[:, src]
            kn = kn * lax.rsqrt(jnp.mean(kn * kn, axis=-1, keepdims=True) + EPS) * kg_ref[...]
            k_ref[0, hd, 0, 0:HEAD_DIM, :] = kn.T.astype(BF16)
            k_ref[0, hd, 0, HEAD_DIM:2 * HEAD_DIM, :] = k_bias_f32[:, bias].T.astype(BF16)


def _fox_proj_call(x, mod, kvmod, norm_g, kv_norm_g, w_in, kv_w, kv_fb, q_norm_g, k_norm_g):
    bsz, s, d = x.shape
    w = w_in.shape[1] // 2
    n_heads = w // HEAD_DIM
    assert (N_SPLIT * n_heads) < HEAD_DIM
    tm = PROJ_TM
    nb = 512
    wq = w_in.astype(BF16)
    wkv = kv_w[:, :2 * w].astype(BF16)
    wf = kv_w[:, 2 * w:]
    wf_hi = wf.astype(BF16)
    wf_lo = (wf - wf_hi.astype(F32)).astype(BF16)
    pad = jnp.zeros((d, HEAD_DIM - n_heads), BF16)
    wf_p = jnp.concatenate([wf_hi, pad, wf_lo, pad], axis=1)
    fb_p = jnp.concatenate([kv_fb, jnp.zeros((HEAD_DIM - n_heads,), F32)]).reshape(1, HEAD_DIM)
    sel_q, sel_k = _bias_selectors(n_heads)
    kern = functools.partial(_fox_proj_kernel, tm=tm, d=d, w=w, nb=nb, n_heads=n_heads)
    tok = lambda b, i: (b, i, 0)
    per_b = lambda b, i: (b, 0, 0)
    act = jax.ShapeDtypeStruct((bsz, s, w), BF16)
    aug = jax.ShapeDtypeStruct((bsz, s, 2 * w), BF16)
    kt_aug = jax.ShapeDtypeStruct((bsz, n_heads, s // tm, 2 * HEAD_DIM, tm), BF16)
    return pl.pallas_call(
        kern,
        grid=(bsz, s // tm),
        in_specs=[pl.BlockSpec((1, tm, d), tok),
                  pl.BlockSpec((1, 1, 3 * d), per_b),
                  pl.BlockSpec((1, 1, 2 * d), per_b),
                  _const_spec((1, d)), _const_spec((1, d)),
                  _const_spec((d, 2 * w)), _const_spec((d, 2 * w)),
                  _const_spec((d, 2 * HEAD_DIM)),
                  _const_spec((1, HEAD_DIM)), _const_spec((1, HEAD_DIM)), _const_spec((1, HEAD_DIM)),
                  _const_spec((HEAD_DIM, w)), _const_spec((HEAD_DIM, w))],
        out_specs=[pl.BlockSpec((1, tm, 2 * w), tok), pl.BlockSpec((1, tm, w), tok),
                   pl.BlockSpec((1, n_heads, 1, 2 * HEAD_DIM, tm), lambda b, i: (b, 0, i, 0, 0)),
                   pl.BlockSpec((1, tm, w), tok)],
        out_shape=[aug, act, kt_aug, act],
        scratch_shapes=[pltpu.VMEM((1, HEAD_DIM), F32)],
        compiler_params=pltpu.CompilerParams(
            dimension_semantics=("parallel", "arbitrary"),
            vmem_limit_bytes=V7X_VMEM_LIMIT_BYTES),
        name="fox_proj",
    )(x, mod.reshape(bsz, 1, 3 * d), kvmod.reshape(bsz, 1, 2 * d), norm_g.reshape(1, d),
      kv_norm_g.reshape(1, d), wq, wkv, wf_p, fb_p, q_norm_g.reshape(1, HEAD_DIM),
      k_norm_g.reshape(1, HEAD_DIM), sel_q, sel_k)


def _fox_attn_kernel(q_ref, k_ref, v_ref, sg_ref, o_ref, acc_ref, *, tq, n_par):
    i = pl.program_id(2)
    aw = 2 * HEAD_DIM
    heads = range(n_par)

    def scores(hh, r0, nr, kblk, nc):
        q = q_ref[0, r0:r0 + nr, hh * aw:(hh + 1) * aw]
        kt = k_ref[0, hh, kblk, :, 0:nc]
        return jnp.dot(q, kt, preferred_element_type=F32)

    def values(hh, c0, nc):
        vb = v_ref[0, pl.ds(c0, nc), hh * HEAD_DIM:(hh + 1) * HEAD_DIM]
        return jnp.concatenate([vb, jnp.ones_like(vb)], axis=1)

    def first_update(s, hh, r0, nr, c0, nc):
        m = jnp.max(s, axis=1, keepdims=True)
        p = jnp.exp2(s - m).astype(BF16)
        acc_ref[hh, r0:r0 + nr, :] = jnp.dot(p, values(hh, c0, nc), preferred_element_type=F32)
        return m

    def update(s, hh, m, c0):
        m_new = jnp.maximum(m, jnp.max(s, axis=1, keepdims=True))
        alpha = jnp.exp2(m - m_new)
        p = jnp.exp2(s - m_new).astype(BF16)
        acc_ref[hh] = alpha * acc_ref[hh] + jnp.dot(p, values(hh, c0, tq),
                                                    preferred_element_type=F32)
        return m_new

    d0 = pl.multiple_of(i * tq, tq)
    causal = (lax.broadcasted_iota(jnp.int32, (tq, tq), 0)
              >= lax.broadcasted_iota(jnp.int32, (tq, tq), 1))
    ss = [jnp.where(causal, scores(hh, 0, tq, i, tq), -jnp.inf) for hh in heads]
    ms = tuple(first_update(ss[hh], hh, 0, tq, d0, tq) for hh in heads)

    def body(j, ms):
        c0 = pl.multiple_of(j * tq, tq)
        ss = [scores(hh, 0, tq, j, tq) for hh in heads]
        return tuple(update(ss[hh], hh, ms[hh], c0) for hh in heads)

    lax.fori_loop(0, i, body, ms)
    for hh in heads:
        acc = acc_ref[hh]
        o = acc[:, 0:HEAD_DIM] / acc[:, HEAD_DIM:aw]
        cols = slice(hh * HEAD_DIM, (hh + 1) * HEAD_DIM)
        o_ref[0, :, cols] = (o * sg_ref[0, :, cols].astype(F32)).astype(BF16)


def _fox_attn_call(q_aug, k_aug, v, sg):
    bsz, s, w = v.shape
    n_par = ATT_HEADS
    n_groups = w // (HEAD_DIM * n_par)
    tq = ATT_TQ
    assert k_aug.shape == (bsz, w // HEAD_DIM, s // tq, 2 * HEAD_DIM, tq)
    kern = functools.partial(_fox_attn_kernel, tq=tq, n_par=n_par)
    blk = lambda b, h, i: (b, i, h)
    whole = lambda b, h, i: (b, 0, h)
    return pl.pallas_call(
        kern,
        grid=(bsz, n_groups, s // tq),
        in_specs=[pl.BlockSpec((1, tq, 2 * HEAD_DIM * n_par), blk),
                  pl.BlockSpec((1, n_par, s // tq, 2 * HEAD_DIM, tq), lambda b, h, i: (b, h, 0, 0, 0)),
                  pl.BlockSpec((1, s, HEAD_DIM * n_par), whole),
                  pl.BlockSpec((1, tq, HEAD_DIM * n_par), blk)],
        out_specs=pl.BlockSpec((1, tq, HEAD_DIM * n_par), blk),
        out_shape=jax.ShapeDtypeStruct((bsz, s, w), BF16),
        scratch_shapes=[pltpu.VMEM((n_par, tq, 2 * HEAD_DIM), F32)],
        compiler_params=pltpu.CompilerParams(
            dimension_semantics=("parallel", "parallel", "arbitrary"),
            vmem_limit_bytes=V7X_VMEM_LIMIT_BYTES),
        name="fox_attn",
    )(q_aug, k_aug, v, sg)


def _out_proj_kernel(x_ref, a_ref, w_ref, mod_ref, o_ref, *, d):
    y = jnp.dot(a_ref[0], w_ref[...], preferred_element_type=F32)
    o_ref[0] = x_ref[0] + mod_ref[0, :, 2 * d:3 * d] * y


def _out_proj_call(x, a, w_out, mod):
    bsz, s, d = x.shape
    w = a.shape[2]
    tm = OUT_TM
    tok = lambda b, i: (b, i, 0)
    return pl.pallas_call(
        functools.partial(_out_proj_kernel, d=d),
        grid=(bsz, s // tm),
        in_specs=[pl.BlockSpec((1, tm, d), tok),
                  pl.BlockSpec((1, tm, w), tok),
                  _const_spec((w, d)),
                  pl.BlockSpec((1, 1, 3 * d), lambda b, i: (b, 0, 0))],
        out_specs=pl.BlockSpec((1, tm, d), tok),
        out_shape=jax.ShapeDtypeStruct((bsz, s, d), F32),
        compiler_params=pltpu.CompilerParams(
            dimension_semantics=("parallel", "parallel"),
            vmem_limit_bytes=V7X_VMEM_LIMIT_BYTES),
        name="out_proj",
    )(x, a, w_out.astype(BF16), mod.reshape(bsz, 1, 3 * d))


def kernel(x, c, mod_w, mod_b, norm_g, a_w_in, a_lb_logits, a_onorm_g, a_w_out,
           kv_mod_w, kv_mod_b, kv_norm_g, kv_w, kv_fb, k_norm_g,
           b_w_in, b_q_norm_g, b_w_out):
    assert mod_w.shape[0] == 2 and a_w_in.shape[0] == 1 and b_w_in.shape[0] == 1
    mod0 = _mod_call(c, mod_w[0], mod_b[0])
    mod1 = _mod_call(c, mod_w[1], mod_b[1])
    kvmod = _mod_call(c, kv_mod_w, kv_mod_b)

    x1 = _hgrn2_call(x, mod0, norm_g[0], a_w_in[0], a_lb_logits, a_onorm_g[0], a_w_out[0])

    q_aug, sg, k_aug, v = _fox_proj_call(x1, mod1, kvmod, norm_g[1], kv_norm_g, b_w_in[0], kv_w,
                                         kv_fb, b_q_norm_g[0], k_norm_g)
    gated = _fox_attn_call(q_aug, k_aug, v, sg)
    return _out_proj_call(x1, gated, b_w_out[0], mod1)
```

```python
import functools

import jax
import jax.numpy as jnp
import numpy as np
from jax import lax
from jax.experimental import pallas as pl
from jax.experimental.pallas import tpu as pltpu

F32 = jnp.float32
BF16 = jnp.bfloat16

HEAD_DIM = 128
CHUNK = 64
EPS = 1e-6
LOG2E = 1.4426950408889634
V7X_VMEM_LIMIT_BYTES = 56 * 1024 * 1024

HG_TM = 512
PROJ_TM = 512
ATT_TQ = 512
ATT_HEADS = 4
OUT_TM = 512


def _sigmoid(x):
    return 0.5 * jnp.tanh(0.5 * x) + 0.5


def _const_spec(shape):
    zeros = (0,) * len(shape)
    return pl.BlockSpec(shape, lambda *_: zeros, pipeline_mode=pl.Buffered(1))


def _mod_kernel(c_ref, w_ref, b_ref, o_ref):
    c = c_ref[...]
    s = c * _sigmoid(c)
    o_ref[...] = jnp.dot(s, w_ref[...], preferred_element_type=F32,
                         precision=lax.Precision.HIGHEST) + b_ref[...]


def _mod_call(c, w, b):
    bsz, d = c.shape
    n = w.shape[1]
    tn = 1024
    return pl.pallas_call(
        _mod_kernel,
        grid=(n // tn,),
        in_specs=[pl.BlockSpec((bsz, d), lambda j: (0, 0)),
                  pl.BlockSpec((d, tn), lambda j: (0, j)),
                  pl.BlockSpec((1, tn), lambda j: (0, j))],
        out_specs=pl.BlockSpec((bsz, tn), lambda j: (0, j)),
        out_shape=jax.ShapeDtypeStruct((bsz, n), F32),
        name="mod",
    )(c, w, b.reshape(1, n))


def _chunk_cumsum(x, rows_in_chunk):
    s = 1
    while s < CHUNK:
        x = x + jnp.where(rows_in_chunk >= s, pltpu.roll(x, s, 0), 0.0)
        s *= 2
    return x


def _hgrn2_kernel(x_ref, mod_ref, ng_ref, win_ref, lbl_ref, og_ref, wout_ref,
                  o_ref, st_ref, h_ref, gated_ref, pa_ref, pb_ref,
                  qd_ref, ki_ref, ks_ref, v_ref, dl_ref, sg_ref, *, tm, d, n_pairs):
    i = pl.program_id(1)
    n_chunks = tm // CHUNK
    pw = 2 * HEAD_DIM

    @pl.when(i == 0)
    def _():
        st_ref[...] = jnp.zeros_like(st_ref)

    x = x_ref[0]
    shift = mod_ref[0, :, 0:d]
    scale = mod_ref[0, :, d:2 * d]
    gate = mod_ref[0, :, 2 * d:3 * d]
    xn = x * lax.rsqrt(jnp.mean(x * x, axis=-1, keepdims=True) + EPS) * ng_ref[...]
    h_ref[...] = (xn * (1.0 + scale) + shift).astype(BF16)

    rows_in_chunk = lax.broadcasted_iota(jnp.int32, (tm, pw), 0) % CHUNK
    tril = (lax.broadcasted_iota(jnp.int32, (CHUNK, CHUNK), 0)
            >= lax.broadcasted_iota(jnp.int32, (CHUNK, CHUNK), 1))

    def in_proj(p, dst_ref):
        dst_ref[...] = jnp.dot(h_ref[...], win_ref[p], preferred_element_type=F32)

    def gates(p, proj_ref):
        q = proj_ref[:, 0:pw]
        fz = proj_ref[:, pw:2 * pw]
        g = proj_ref[:, 3 * pw:4 * pw]

        l0 = lbl_ref[0, p]
        l1 = lbl_ref[1, p]
        mx = jnp.maximum(l0, l1)
        e0 = jnp.exp(l0 - mx)
        e1 = jnp.exp(l1 - mx)
        lb = e0 / (e0 + e1)

        sig = _sigmoid(fz)
        logf = jnp.log(lb + (1.0 - lb) * sig)
        k = (1.0 - lb) * (1.0 - sig)
        b = _chunk_cumsum(logf, rows_in_chunk)
        b3 = b.reshape(n_chunks, CHUNK, pw)
        b_last = b3[:, CHUNK - 1:CHUNK, :]
        qd_ref[...] = (q * jnp.exp(b)).astype(BF16)
        ki_ref[...] = (k * jnp.exp(-b)).astype(BF16)
        ks = k.reshape(n_chunks, CHUNK, pw) * jnp.exp(b_last - b3)
        ks_ref[...] = ks.reshape(tm, pw).astype(BF16)
        v_ref[...] = proj_ref[:, 2 * pw:3 * pw].astype(BF16)
        dl_ref[...] = jnp.exp(b_last)
        sg_ref[...] = g * _sigmoid(g)

    def chunks(p):
        normed = []
        for hh in range(2):
            sl = slice(hh * HEAD_DIM, (hh + 1) * HEAD_DIM)
            st = st_ref[2 * p + hh]
            outs = []
            for c in range(n_chunks):
                rows = slice(c * CHUNK, (c + 1) * CHUNK)
                qd_c = qd_ref[rows, sl]
                v_c = v_ref[rows, sl]
                att = lax.dot_general(qd_c, ki_ref[rows, sl], (((1,), (1,)), ((), ())),
                                      preferred_element_type=F32)
                att = jnp.where(tril, att, 0.0).astype(BF16)
                o_c = jnp.dot(att, v_c, preferred_element_type=F32)
                o_c = o_c + lax.dot_general(qd_c, st.astype(BF16), (((1,), (1,)), ((), ())),
                                            preferred_element_type=F32)
                d_st = lax.dot_general(v_c, ks_ref[rows, sl], (((0,), (0,)), ((), ())),
                                       preferred_element_type=F32)
                st = st * dl_ref[c][:, sl] + d_st
                outs.append(o_c)
            st_ref[2 * p + hh] = st
            o_h = jnp.concatenate(outs, axis=0)
            o_h = o_h * lax.rsqrt(jnp.mean(o_h * o_h, axis=-1, keepdims=True) + EPS)
            normed.append(o_h)
        o_pair = jnp.concatenate(normed, axis=1) * og_ref[p]
        gated_ref[p] = (o_pair * sg_ref[...]).astype(BF16)

    def two_pairs(t, carry):
        in_proj(2 * t + 1, pb_ref)
        gates(2 * t, pa_ref)
        chunks(2 * t)
        in_proj(2 * t + 2, pa_ref)
        gates(2 * t + 1, pb_ref)
        chunks(2 * t + 1)
        return carry

    in_proj(0, pa_ref)
    lax.fori_loop(0, n_pairs // 2 - 1, two_pairs, 0)
    in_proj(n_pairs - 1, pb_ref)
    gates(n_pairs - 2, pa_ref)
    chunks(n_pairs - 2)
    gates(n_pairs - 1, pb_ref)
    chunks(n_pairs - 1)

    y = jnp.dot(gated_ref[0], wout_ref[0], preferred_element_type=F32)
    for p in range(1, n_pairs):
        y = y + jnp.dot(gated_ref[p], wout_ref[p], preferred_element_type=F32)
    o_ref[0] = x + gate * y


def _hgrn2_call(x, mod, norm_g, w_in, lb_logits, onorm_g, w_out):
    bsz, s, d = x.shape
    w = w_out.shape[0]
    n_heads = w // HEAD_DIM
    n_pairs = n_heads // 2
    pw = 2 * HEAD_DIM
    tm = HG_TM
    w_in_p = (w_in.reshape(d, 4, n_pairs, pw).transpose(2, 0, 1, 3)
              .reshape(n_pairs, d, 4 * pw).astype(BF16))
    w_out_p = w_out.reshape(n_pairs, pw, d).astype(BF16)
    kern = functools.partial(_hgrn2_kernel, tm=tm, d=d, n_pairs=n_pairs)
    return pl.pallas_call(
        kern,
        grid=(bsz, s // tm),
        in_specs=[pl.BlockSpec((1, tm, d), lambda b, i: (b, i, 0)),
                  pl.BlockSpec((1, 1, 3 * d), lambda b, i: (b, 0, 0)),
                  _const_spec((1, d)),
                  _const_spec((n_pairs, d, 4 * pw)),
                  _const_spec((2, n_pairs, 1, pw)),
                  _const_spec((n_pairs, 1, pw)),
                  _const_spec((n_pairs, pw, d))],
        out_specs=pl.BlockSpec((1, tm, d), lambda b, i: (b, i, 0)),
        out_shape=jax.ShapeDtypeStruct((bsz, s, d), F32),
        scratch_shapes=[pltpu.VMEM((n_heads, HEAD_DIM, HEAD_DIM), F32),
                        pltpu.VMEM((tm, d), BF16),
                        pltpu.VMEM((n_pairs, tm, pw), BF16),
                        pltpu.VMEM((tm, 4 * pw), F32),
                        pltpu.VMEM((tm, 4 * pw), F32),
                        pltpu.VMEM((tm, pw), BF16),
                        pltpu.VMEM((tm, pw), BF16),
                        pltpu.VMEM((tm, pw), BF16),
                        pltpu.VMEM((tm, pw), BF16),
                        pltpu.VMEM((tm // CHUNK, 1, pw), F32),
                        pltpu.VMEM((tm, pw), F32)],
        compiler_params=pltpu.CompilerParams(
            dimension_semantics=("parallel", "arbitrary"),
            vmem_limit_bytes=V7X_VMEM_LIMIT_BYTES),
        name="hgrn2_layer",
    )(x, mod.reshape(bsz, 1, 3 * d), norm_g.reshape(1, d), w_in_p,
      lb_logits.reshape(2, n_pairs, 1, pw), onorm_g.reshape(n_pairs, 1, pw), w_out_p)


N_SPLIT = 3


def _row_cumsum(x, n_rows):
    rows = lax.broadcasted_iota(jnp.int32, x.shape, 0)
    s = 1
    while s < n_rows:
        x = x + jnp.where(rows >= s, pltpu.roll(x, s, 0), 0.0)
        s *= 2
    return x


def _bias_selectors(n_heads):
    one_row = N_SPLIT * n_heads
    sel_k = np.zeros((HEAD_DIM, n_heads * HEAD_DIM), np.float32)
    sel_q = np.zeros((HEAD_DIM, n_heads * HEAD_DIM), np.float32)
    for h in range(n_heads):
        for r in range(N_SPLIT):
            sel_k[r * n_heads + h, h * HEAD_DIM + r] = -1.0
            sel_k[one_row, h * HEAD_DIM + N_SPLIT + r] = 1.0
            sel_q[one_row, h * HEAD_DIM + r] = 1.0
            sel_q[r * n_heads + h, h * HEAD_DIM + N_SPLIT + r] = 1.0
    return jnp.asarray(sel_q, BF16), jnp.asarray(sel_k, BF16)


def _fox_proj_kernel(x_ref, mod_ref, kvmod_ref, ng_ref, kvng_ref, wq_ref, wkv_ref, wf_ref,
                     fb_ref, qg_ref, kg_ref, selq_ref, selk_ref,
                     q_ref, sg_ref, k_ref, v_ref, carry_ref, *, tm, d, w, nb, n_heads):
    i = pl.program_id(1)

    @pl.when(i == 0)
    def _():
        carry_ref[...] = jnp.zeros_like(carry_ref)

    x = x_ref[0]
    xn = x * lax.rsqrt(jnp.mean(x * x, axis=-1, keepdims=True) + EPS)
    h = ((xn * ng_ref[...]) * (1.0 + mod_ref[0, :, d:2 * d]) + mod_ref[0, :, 0:d]).astype(BF16)
    hkv = (xn * kvng_ref[...]) * (1.0 + kvmod_ref[0, :, d:2 * d]) + kvmod_ref[0, :, 0:d]
    hkv_hi = hkv.astype(BF16)
    hkv_lo = (hkv - hkv_hi.astype(F32)).astype(BF16)

    r1 = jnp.dot(hkv_hi, wf_ref[...], preferred_element_type=F32)
    r2 = jnp.dot(hkv_lo, wf_ref[:, 0:HEAD_DIM], preferred_element_type=F32)
    fl = r1[:, 0:HEAD_DIM] + r1[:, HEAD_DIM:2 * HEAD_DIM] + r2 + fb_ref[...]
    log_sig = jnp.minimum(fl, 0.0) - jnp.log(1.0 + jnp.exp(-jnp.abs(fl)))
    cum = _row_cumsum(log_sig, tm) + carry_ref[...]
    carry_ref[...] = cum[tm - 1:tm, :]

    f2 = cum * LOG2E
    lane = lax.broadcasted_iota(jnp.int32, f2.shape, 1)
    split = jnp.where(lane == N_SPLIT * n_heads, 1.0, 0.0)
    rest = f2
    for r in range(N_SPLIT):
        piece = rest.astype(BF16).astype(F32)
        rest = rest - piece
        placed = piece if r == 0 else pltpu.roll(piece, r * n_heads, 1)
        split = jnp.where((lane >= r * n_heads) & (lane < (r + 1) * n_heads), placed, split)
    split = split.astype(BF16)
    q_bias = jnp.dot(split, selq_ref[...], preferred_element_type=F32).astype(BF16)
    k_bias_f32 = jnp.dot(split, selk_ref[...], preferred_element_type=F32)

    def head_norm(t, gain):
        th = t * lax.rsqrt(jnp.mean(t * t, axis=-1, keepdims=True) + EPS) * gain
        return th.astype(BF16)

    q_gain = qg_ref[...] * (HEAD_DIM ** -0.5 * LOG2E)
    for j in range(w // nb):
        cols = slice(j * nb, (j + 1) * nb)
        qj = jnp.dot(h, wq_ref[:, cols], preferred_element_type=F32)
        gj = jnp.dot(h, wq_ref[:, w + j * nb:w + (j + 1) * nb], preferred_element_type=F32)
        sg_ref[0, :, cols] = (gj * _sigmoid(gj)).astype(BF16)
        kj = jnp.dot(hkv_hi, wkv_ref[:, cols], preferred_element_type=F32)
        vj = jnp.dot(hkv_hi, wkv_ref[:, w + j * nb:w + (j + 1) * nb], preferred_element_type=F32)
        v_ref[0, :, cols] = vj.astype(BF16)
        for t in range(nb // HEAD_DIM):
            hd = j * (nb // HEAD_DIM) + t
            src = slice(t * HEAD_DIM, (t + 1) * HEAD_DIM)
            bias = slice(hd * HEAD_DIM, (hd + 1) * HEAD_DIM)
            q_ref[0, :, 2 * hd * HEAD_DIM:(2 * hd + 1) * HEAD_DIM] = head_norm(qj[:, src], q_gain)
            q_ref[0, :, (2 * hd + 1) * HEAD_DIM:(2 * hd + 2) * HEAD_DIM] = q_bias[:, bias]
            kn = kj[:, src]
            kn = kn * lax.rsqrt(jnp.mean(kn * kn, axis=-1, keepdims=True) + EPS) * kg_ref[...]
            k_ref[0, hd, 0, 0:HEAD_DIM, :] = kn.T.astype(BF16)
            k_ref[0, hd, 0, HEAD_DIM:2 * HEAD_DIM, :] = k_bias_f32[:, bias].T.astype(BF16)


def _fox_proj_call(x, mod, kvmod, norm_g, kv_norm_g, w_in, kv_w, kv_fb, q_norm_g, k_norm_g):
    bsz, s, d = x.shape
    w = w_in.shape[1] // 2
    n_heads = w // HEAD_DIM
    assert (N_SPLIT * n_heads) < HEAD_DIM
    tm = PROJ_TM
    nb = 512
    wq = w_in.astype(BF16)
    wkv = kv_w[:, :2 * w].astype(BF16)
    wf = kv_w[:, 2 * w:]
    wf_hi = wf.astype(BF16)
    wf_lo = (wf - wf_hi.astype(F32)).astype(BF16)
    pad = jnp.zeros((d, HEAD_DIM - n_heads), BF16)
    wf_p = jnp.concatenate([wf_hi, pad, wf_lo, pad], axis=1)
    fb_p = jnp.concatenate([kv_fb, jnp.zeros((HEAD_DIM - n_heads,), F32)]).reshape(1, HEAD_DIM)
    sel_q, sel_k = _bias_selectors(n_heads)
    kern = functools.partial(_fox_proj_kernel, tm=tm, d=d, w=w, nb=nb, n_heads=n_heads)
    tok = lambda b, i: (b, i, 0)
    per_b = lambda b, i: (b, 0, 0)
    act = jax.ShapeDtypeStruct((bsz, s, w), BF16)
    aug = jax.ShapeDtypeStruct((bsz, s, 2 * w), BF16)
    kt_aug = jax.ShapeDtypeStruct((bsz, n_heads, s // tm, 2 * HEAD_DIM, tm), BF16)
    return pl.pallas_call(
        kern,
        grid=(bsz, s // tm),
        in_specs=[pl.BlockSpec((1, tm, d), tok),
                  pl.BlockSpec((1, 1, 3 * d), per_b),
                  pl.BlockSpec((1, 1, 2 * d), per_b),
                  _const_spec((1, d)), _const_spec((1, d)),
                  _const_spec((d, 2 * w)), _const_spec((d, 2 * w)),
                  _const_spec((d, 2 * HEAD_DIM)),
                  _const_spec((1, HEAD_DIM)), _const_spec((1, HEAD_DIM)), _const_spec((1, HEAD_DIM)),
                  _const_spec((HEAD_DIM, w)), _const_spec((HEAD_DIM, w))],
        out_specs=[pl.BlockSpec((1, tm, 2 * w), tok), pl.BlockSpec((1, tm, w), tok),
                   pl.BlockSpec((1, n_heads, 1, 2 * HEAD_DIM, tm), lambda b, i: (b, 0, i, 0, 0)),
                   pl.BlockSpec((1, tm, w), tok)],
        out_shape=[aug, act, kt_aug, act],
        scratch_shapes=[pltpu.VMEM((1, HEAD_DIM), F32)],
        compiler_params=pltpu.CompilerParams(
            dimension_semantics=("parallel", "arbitrary"),
            vmem_limit_bytes=V7X_VMEM_LIMIT_BYTES),
        name="fox_proj",
    )(x, mod.reshape(bsz, 1, 3 * d), kvmod.reshape(bsz, 1, 2 * d), norm_g.reshape(1, d),
      kv_norm_g.reshape(1, d), wq, wkv, wf_p, fb_p, q_norm_g.reshape(1, HEAD_DIM),
      k_norm_g.reshape(1, HEAD_DIM), sel_q, sel_k)


def _fox_attn_kernel(q_ref, k_ref, v_ref, sg_ref, o_ref, acc_ref, *, tq, n_par):
    i = pl.program_id(2)
    aw = 2 * HEAD_DIM
    heads = range(n_par)

    def scores(hh, r0, nr, kblk, nc):
        q = q_ref[0, r0:r0 + nr, hh * aw:(hh + 1) * aw]
        kt = k_ref[0, hh, kblk, :, 0:nc]
        return jnp.dot(q, kt, preferred_element_type=F32)

    def values(hh, c0, nc):
        vb = v_ref[0, pl.ds(c0, nc), hh * HEAD_DIM:(hh + 1) * HEAD_DIM]
        return jnp.concatenate([vb, jnp.ones_like(vb)], axis=1)

    def first_update(s, hh, r0, nr, c0, nc):
        m = jnp.max(s, axis=1, keepdims=True)
        p = jnp.exp2((s - m).astype(BF16))
        acc_ref[hh, r0:r0 + nr, :] = jnp.dot(p, values(hh, c0, nc), preferred_element_type=F32)
        return m

    def update(s, hh, m, c0):
        m_new = jnp.maximum(m, jnp.max(s, axis=1, keepdims=True))
        alpha = jnp.exp2(m - m_new)
        p = jnp.exp2((s - m_new).astype(BF16))
        acc_ref[hh] = alpha * acc_ref[hh] + jnp.dot(p, values(hh, c0, tq),
                                                    preferred_element_type=F32)
        return m_new

    d0 = pl.multiple_of(i * tq, tq)
    causal = (lax.broadcasted_iota(jnp.int32, (tq, tq), 0)
              >= lax.broadcasted_iota(jnp.int32, (tq, tq), 1))
    ss = [jnp.where(causal, scores(hh, 0, tq, i, tq), -jnp.inf) for hh in heads]
    ms = tuple(first_update(ss[hh], hh, 0, tq, d0, tq) for hh in heads)

    def body(j, ms):
        c0 = pl.multiple_of(j * tq, tq)
        ss = [scores(hh, 0, tq, j, tq) for hh in heads]
        return tuple(update(ss[hh], hh, ms[hh], c0) for hh in heads)

    lax.fori_loop(0, i, body, ms)
    for hh in heads:
        acc = acc_ref[hh]
        o = acc[:, 0:HEAD_DIM] / acc[:, HEAD_DIM:aw]
        cols = slice(hh * HEAD_DIM, (hh + 1) * HEAD_DIM)
        o_ref[0, :, cols] = (o * sg_ref[0, :, cols].astype(F32)).astype(BF16)


def _fox_attn_call(q_aug, k_aug, v, sg):
    bsz, s, w = v.shape
    n_par = ATT_HEADS
    n_groups = w // (HEAD_DIM * n_par)
    tq = ATT_TQ
    assert k_aug.shape == (bsz, w // HEAD_DIM, s // tq, 2 * HEAD_DIM, tq)
    kern = functools.partial(_fox_attn_kernel, tq=tq, n_par=n_par)
    blk = lambda b, h, i: (b, i, h)
    whole = lambda b, h, i: (b, 0, h)
    return pl.pallas_call(
        kern,
        grid=(bsz, n_groups, s // tq),
        in_specs=[pl.BlockSpec((1, tq, 2 * HEAD_DIM * n_par), blk),
                  pl.BlockSpec((1, n_par, s // tq, 2 * HEAD_DIM, tq), lambda b, h, i: (b, h, 0, 0, 0)),
                  pl.BlockSpec((1, s, HEAD_DIM * n_par), whole),
                  pl.BlockSpec((1, tq, HEAD_DIM * n_par), blk)],
        out_specs=pl.BlockSpec((1, tq, HEAD_DIM * n_par), blk),
        out_shape=jax.ShapeDtypeStruct((bsz, s, w), BF16),
        scratch_shapes=[pltpu.VMEM((n_par, tq, 2 * HEAD_DIM), F32)],
        compiler_params=pltpu.CompilerParams(
            dimension_semantics=("parallel", "parallel", "arbitrary"),
            vmem_limit_bytes=V7X_VMEM_LIMIT_BYTES),
        name="fox_attn",
    )(q_aug, k_aug, v, sg)


def _out_proj_kernel(x_ref, a_ref, w_ref, mod_ref, o_ref, *, d):
    y = jnp.dot(a_ref[0], w_ref[...], preferred_element_type=F32)
    o_ref[0] = x_ref[0] + mod_ref[0, :, 2 * d:3 * d] * y


def _out_proj_call(x, a, w_out, mod):
    bsz, s, d = x.shape
    w = a.shape[2]
    tm = OUT_TM
    tok = lambda b, i: (b, i, 0)
    return pl.pallas_call(
        functools.partial(_out_proj_kernel, d=d),
        grid=(bsz, s // tm),
        in_specs=[pl.BlockSpec((1, tm, d), tok),
                  pl.BlockSpec((1, tm, w), tok),
                  _const_spec((w, d)),
                  pl.BlockSpec((1, 1, 3 * d), lambda b, i: (b, 0, 0))],
        out_specs=pl.BlockSpec((1, tm, d), tok),
        out_shape=jax.ShapeDtypeStruct((bsz, s, d), F32),
        compiler_params=pltpu.CompilerParams(
            dimension_semantics=("parallel", "parallel"),
            vmem_limit_bytes=V7X_VMEM_LIMIT_BYTES),
        name="out_proj",
    )(x, a, w_out.astype(BF16), mod.reshape(bsz, 1, 3 * d))


def kernel(x, c, mod_w, mod_b, norm_g, a_w_in, a_lb_logits, a_onorm_g, a_w_out,
           kv_mod_w, kv_mod_b, kv_norm_g, kv_w, kv_fb, k_norm_g,
           b_w_in, b_q_norm_g, b_w_out):
    assert mod_w.shape[0] == 2 and a_w_in.shape[0] == 1 and b_w_in.shape[0] == 1
    mod0 = _mod_call(c, mod_w[0], mod_b[0])
    mod1 = _mod_call(c, mod_w[1], mod_b[1])
    kvmod = _mod_call(c, kv_mod_w, kv_mod_b)

    x1 = _hgrn2_call(x, mod0, norm_g[0], a_w_in[0], a_lb_logits, a_onorm_g[0], a_w_out[0])

    q_aug, sg, k_aug, v = _fox_proj_call(x1, mod1, kvmod, norm_g[1], kv_norm_g, b_w_in[0], kv_w,
                                         kv_fb, b_q_norm_g[0], k_norm_g)
    gated = _fox_attn_call(q_aug, k_aug, v, sg)
    return _out_proj_call(x1, gated, b_w_out[0], mod1)
```
